```python
import math
import jax, jax.numpy as jnp
from jax import lax
import numpy as np

D_MODEL = 1024
BATCH = 4
SEQ = 8192
DEPTH = 2
DEC_BATCH = 128
DEC_SEQ = 4
PAST_LEN = 16384
PAGE_SIZE = 128

N_EVEN = (DEPTH + 1) // 2
N_ODD = DEPTH // 2

A_HEADS = 8
A_KV_HEADS = 2
A_GROUP = A_HEADS // A_KV_HEADS
A_HEAD_DIM = 64
A_WIDTH = A_HEADS * A_HEAD_DIM
MOBA_BLOCK = 256
MOBA_TOPK = 3
MOBA_QBLOCK = 32

B_HEADS = 8
B_NOPE = 64
B_ROPE = 32
B_VDIM = 64
B_KV_RANK = 256
B_WIDTH = B_HEADS * B_VDIM
ROPE_THETA = 10000.0

C_HEADS = A_HEADS
C_KV_HEADS = 2
C_GROUP = C_HEADS // C_KV_HEADS
C_HEAD_DIM = 128
C_WIDTH = C_HEADS * C_HEAD_DIM
IDX_HEADS = 8
IDX_DIM = 64
DSA_TOPK = 256

REL_BUCKETS = 32
REL_MAX_DIST = 128
REL_HEADS = A_HEADS

QBLOCK = 128
EPS = 1e-6

EVEN_SIZES = (A_WIDTH, A_KV_HEADS * A_HEAD_DIM, A_KV_HEADS * A_HEAD_DIM, A_WIDTH,
              B_HEADS * B_NOPE, B_HEADS * B_ROPE, B_KV_RANK, B_ROPE, B_WIDTH)
ODD_SIZES = (C_WIDTH, C_KV_HEADS * C_HEAD_DIM, C_KV_HEADS * C_HEAD_DIM, C_WIDTH,
             IDX_HEADS * IDX_DIM, IDX_DIM, IDX_HEADS)

kernel_name = 'moba_mla_dsa_adaln_decoder_step'


def _split(z, sizes):
    return jnp.split(z, np.cumsum(sizes)[:-1].tolist(), axis=-1)


def rms_norm(x, g):
    xf = x.astype(jnp.float32)
    xf = xf * lax.rsqrt(jnp.mean(xf * xf, axis=-1, keepdims=True) + EPS)
    return xf.astype(x.dtype) * g


def rope(x, pos):
    half = x.shape[-1] // 2
    inv = ROPE_THETA ** (-jnp.arange(half, dtype=jnp.float32) / half)
    ang = pos.astype(jnp.float32)[:, None] * inv[None, :]
    ang = ang.reshape((ang.shape[0],) + (1,) * (x.ndim - 2) + (half,))
    cos, sin = jnp.cos(ang), jnp.sin(ang)
    xf = x.astype(jnp.float32)
    x1, x2 = xf[..., :half], xf[..., half:]
    return jnp.concatenate([x1 * cos - x2 * sin, x1 * sin + x2 * cos], axis=-1).astype(x.dtype)


def t5_bucket(dist):
    n = jnp.maximum(dist, 0)
    exact = REL_BUCKETS // 2
    nf = jnp.maximum(n, 1).astype(jnp.float32)
    big = exact + (jnp.log(nf / exact) / math.log(REL_MAX_DIST / exact) * (REL_BUCKETS - exact)).astype(jnp.int32)
    return jnp.where(n < exact, n, jnp.minimum(big, REL_BUCKETS - 1))


def moba_one(q, k, v, q_pos, rel_bias):
    L = k.shape[0]
    nb = -(-L // MOBA_BLOCK)
    pad = nb * MOBA_BLOCK - L
    k = jnp.pad(k, ((0, pad), (0, 0), (0, 0)))
    v = jnp.pad(v, ((0, pad), (0, 0), (0, 0)))
    kb = k.reshape(nb, MOBA_BLOCK, A_KV_HEADS, A_HEAD_DIM).transpose(2, 0, 1, 3)
    vb = v.reshape(nb, MOBA_BLOCK, A_KV_HEADS, A_HEAD_DIM).transpose(2, 0, 1, 3)
    kmean = jnp.mean(kb.astype(jnp.float32), axis=2)
    kk = min(MOBA_TOPK, nb)
    tq = q.shape[0]
    qb = math.gcd(tq, MOBA_QBLOCK)
    nq = tq // qb
    qs = q.reshape(nq, qb, A_KV_HEADS, A_GROUP, A_HEAD_DIM).transpose(0, 2, 3, 1, 4)
    ps = q_pos.reshape(nq, qb)
    kvh_ix = jnp.arange(A_KV_HEADS)[:, None, None, None]
    head_ix = (jnp.arange(A_KV_HEADS)[:, None] * A_GROUP + jnp.arange(A_GROUP)[None, :])[:, :, None, None, None]
    offs = jnp.arange(MOBA_BLOCK)
    scale = A_HEAD_DIM ** -0.5

    def block(a):
        qg, pos = a
        own = pos // MOBA_BLOCK
        gate = jnp.einsum('kgtd,knd->kgtn', qg.astype(jnp.float32), kmean)
        gate = jnp.where(jnp.arange(nb)[None, :] < own[:, None], gate, -jnp.inf)
        _, top = lax.top_k(gate, kk)
        idx = jnp.concatenate([top, jnp.broadcast_to(own[:, None], (A_KV_HEADS, A_GROUP, qb, 1))], axis=-1)
        slot_ok = jnp.concatenate([jnp.arange(kk)[None, :] < own[:, None], jnp.ones((qb, 1), bool)], axis=-1)
        ksel = kb[kvh_ix, idx]
        vsel = vb[kvh_ix, idx]
        kpos = idx[..., None] * MOBA_BLOCK + offs
        logits = jnp.einsum('kgtd,kgtsjd->kgtsj', qg, ksel).astype(jnp.float32) * scale
        bias = rel_bias[t5_bucket(pos[:, None, None] - kpos), head_ix].astype(jnp.float32)
        ok = slot_ok[:, :, None] & (kpos <= pos[:, None, None])
        logits = jnp.where(ok, logits + bias, -jnp.inf)
        p = jax.nn.softmax(logits.reshape(logits.shape[:3] + (-1,)), axis=-1).reshape(logits.shape).astype(vb.dtype)
        o = jnp.einsum('kgtsj,kgtsjd->tkgd', p, vsel)
        return o.reshape(qb, A_HEADS, A_HEAD_DIM)

    return lax.map(block, (qs, ps)).reshape(tq, A_HEADS, A_HEAD_DIM)


def dense_causal_one(q, k, v, q_pos):
    tq, h, d = q.shape
    L = k.shape[0]
    qb = math.gcd(tq, QBLOCK)
    nq = tq // qb
    kpos = jnp.arange(L)
    scale = d ** -0.5

    def block(a):
        qq, pos = a
        s = jnp.einsum('thd,lhd->htl', qq, k).astype(jnp.float32) * scale
        s = jnp.where(kpos[None, None, :] <= pos[None, :, None], s, -jnp.inf)
        p = jax.nn.softmax(s, axis=-1).astype(v.dtype)
        return jnp.einsum('htl,lhd->thd', p, v)

    o = lax.map(block, (q.reshape(nq, qb, h, d), q_pos.reshape(nq, qb)))
    return o.reshape(tq, h, v.shape[-1])


def mla_one(q_nope, q_pe, ckv, kpe, q_pos, w_uk, w_uv, g_q, g_k):
    L = ckv.shape[0]
    k_nope = jnp.einsum('lr,rhd->lhd', ckv, w_uk)
    v = jnp.einsum('lr,rhd->lhd', ckv, w_uv)
    q = rms_norm(jnp.concatenate([q_nope, q_pe], axis=-1), g_q)
    k = rms_norm(jnp.concatenate([k_nope, jnp.broadcast_to(kpe[:, None, :], (L, B_HEADS, B_ROPE))], axis=-1), g_k)
    q = jnp.concatenate([q[..., :B_NOPE], rope(q[..., B_NOPE:], q_pos)], axis=-1)
    k = jnp.concatenate([k[..., :B_NOPE], rope(k[..., B_NOPE:], jnp.arange(L))], axis=-1)
    return dense_causal_one(q, k, v, q_pos)


def dsa_one(q, qi, wi, k, v, ki, q_pos, rel_bias):
    L = k.shape[0]
    topk = min(DSA_TOPK, L // 4)
    tq = q.shape[0]
    qb = math.gcd(tq, QBLOCK)
    nq = tq // qb
    kpos = jnp.arange(L)
    scale = C_HEAD_DIM ** -0.5

    def block(a):
        qq, qiq, wiq, pos = a
        s = jnp.einsum('thd,ld->thl', qiq, ki).astype(jnp.float32) * IDX_DIM ** -0.5
        score = jnp.einsum('th,thl->tl', wiq.astype(jnp.float32), jax.nn.relu(s))
        score = jnp.where(kpos[None, :] <= pos[:, None], score, -jnp.inf)
        _, sel = lax.top_k(score, topk)
        ok = sel <= pos[:, None]
        ks = k[sel]
        vs = v[sel]
        qg = qq.reshape(qb, C_KV_HEADS, C_GROUP, C_HEAD_DIM)
        logits = jnp.einsum('tkgd,tjkd->kgtj', qg, ks).astype(jnp.float32) * scale
        bias = rel_bias[t5_bucket(pos[:, None] - sel)].astype(jnp.float32)
        bias = bias.transpose(2, 0, 1).reshape(C_KV_HEADS, C_GROUP, qb, topk)
        logits = jnp.where(ok, logits + bias, -jnp.inf)
        p = jax.nn.softmax(logits, axis=-1).astype(v.dtype)
        o = jnp.einsum('kgtj,tjkd->tkgd', p, vs)
        return o.reshape(qb, C_HEADS, C_HEAD_DIM)

    args = (q.reshape(nq, qb, C_HEADS, C_HEAD_DIM), qi.reshape(nq, qb, IDX_HEADS, IDX_DIM),
            wi.reshape(nq, qb, IDX_HEADS), q_pos.reshape(nq, qb))
    return lax.map(block, args).reshape(tq, C_HEADS, C_HEAD_DIM)


def per_sequence(fn, queries, new_rows, pools, layer, page_table):
    if page_table is None:
        return lax.map(lambda a: fn(*a[0], *a[1]), (queries, new_rows))

    def one(a):
        qs, news, pt = a
        keys = tuple(jnp.concatenate([pool[layer, pt].reshape((-1,) + pool.shape[3:]), nr], axis=0)
                     for pool, nr in zip(pools, news))
        return fn(*qs, *keys)

    return lax.map(one, (queries, new_rows, page_table))


def trunk(x, c, page_table, pools_even, pools_odd, weights):
    (rel_bias, norm_g, ada_w, ada_b, w_in_even, ga_q, ga_k, gb_q, gb_k, g_ckv, w_uk, w_uv,
     w_out_even, w_in_odd, gc_q, gc_k, w_out_odd) = weights
    nbatch, t, _ = x.shape
    past = 0 if page_table is None else page_table.shape[1] * PAGE_SIZE
    pos = past + jnp.arange(t, dtype=jnp.int32)
    rows_even, rows_odd = [], []
    for l in range(DEPTH):
        shift, scale, gate = jnp.split(jax.nn.silu(c) @ ada_w[l] + ada_b[l], 3, axis=-1)
        h = rms_norm(x, norm_g[l]) * (1 + scale[:, None, :]) + shift[:, None, :]
        i = l // 2
        if l % 2 == 0:
            qa, ka, va, gta, qbn, qbp, ckv, kpe, gtb = _split(h @ w_in_even[i], EVEN_SIZES)
            qa = rms_norm(qa.reshape(nbatch, t, A_HEADS, A_HEAD_DIM), ga_q[i])
            ka = rms_norm(ka.reshape(nbatch, t, A_KV_HEADS, A_HEAD_DIM), ga_k[i])
            va = va.reshape(nbatch, t, A_KV_HEADS, A_HEAD_DIM)
            qbn = qbn.reshape(nbatch, t, B_HEADS, B_NOPE)
            qbp = qbp.reshape(nbatch, t, B_HEADS, B_ROPE)
            ckv = rms_norm(ckv, g_ckv[i])
            oa = per_sequence(lambda q, k, v: moba_one(q, k, v, pos, rel_bias),
                              (qa,), (ka, va), pools_even[:2], i, page_table)
            ob = per_sequence(lambda qn, qp, cc, kp: mla_one(qn, qp, cc, kp, pos, w_uk[i], w_uv[i], gb_q[i], gb_k[i]),
                              (qbn, qbp), (ckv, kpe), pools_even[2:], i, page_table)
            mixed = jnp.concatenate([oa.reshape(nbatch, t, A_WIDTH) * jax.nn.silu(gta),
                                     ob.reshape(nbatch, t, B_WIDTH) * jax.nn.silu(gtb)], axis=-1)
            out = mixed @ w_out_even[i]
            rows_even.append((ka, va, ckv, kpe))
        else:
            qc, kc, vc, gtc, qi, ki, wi = _split(h @ w_in_odd[i], ODD_SIZES)
            qc = rms_norm(qc.reshape(nbatch, t, C_HEADS, C_HEAD_DIM), gc_q[i])
            kc = rms_norm(kc.reshape(nbatch, t, C_KV_HEADS, C_HEAD_DIM), gc_k[i])
            vc = vc.reshape(nbatch, t, C_KV_HEADS, C_HEAD_DIM)
            qi = qi.reshape(nbatch, t, IDX_HEADS, IDX_DIM)
            wi = wi * IDX_HEADS ** -0.5
            oc = per_sequence(lambda q, qq, ww, k, v, kx: dsa_one(q, qq, ww, k, v, kx, pos, rel_bias),
                              (qc, qi, wi), (kc, vc, ki), pools_odd, i, page_table)
            out = (oc.reshape(nbatch, t, C_WIDTH) * jax.nn.silu(gtc)) @ w_out_odd[i]
            rows_odd.append((kc, vc, ki))
        x = x + gate[:, None, :] * out
    return x, rows_even, rows_odd


def setup_inputs(seed: int = 0) -> dict:
    key = jax.random.key(seed)
    ks = jax.random.split(key, 32)
    n_pages = PAST_LEN // PAGE_SIZE
    n_pool = (DEC_BATCH * n_pages * 5) // 4
    f32 = jnp.float32

    def nrm(k, shape, s=1.0):
        return jax.random.normal(k, shape, f32) * s

    def gain(k, shape):
        return 1.0 + 0.05 * jax.random.normal(k, shape, f32)

    page_table = jax.random.permutation(ks[0], n_pool)[:DEC_BATCH * n_pages].reshape(DEC_BATCH, n_pages).astype(jnp.int32)
    even_in = sum(EVEN_SIZES)
    odd_in = sum(ODD_SIZES)
    return {
        'x_prompt': nrm(ks[1], (BATCH, SEQ, D_MODEL)),
        'x_sample': nrm(ks[2], (DEC_BATCH, DEC_SEQ, D_MODEL)),
        'cache_moba_k': nrm(ks[3], (N_EVEN, n_pool, PAGE_SIZE, A_KV_HEADS, A_HEAD_DIM)),
        'cache_moba_v': nrm(ks[4], (N_EVEN, n_pool, PAGE_SIZE, A_KV_HEADS, A_HEAD_DIM)),
        'cache_mla_ckv': nrm(ks[5], (N_EVEN, n_pool, PAGE_SIZE, B_KV_RANK)),
        'cache_mla_kpe': nrm(ks[6], (N_EVEN, n_pool, PAGE_SIZE, B_ROPE)),
        'cache_dsa_k': nrm(ks[7], (N_ODD, n_pool, PAGE_SIZE, C_KV_HEADS, C_HEAD_DIM)),
        'cache_dsa_v': nrm(ks[8], (N_ODD, n_pool, PAGE_SIZE, C_KV_HEADS, C_HEAD_DIM)),
        'cache_dsa_kidx': nrm(ks[9], (N_ODD, n_pool, PAGE_SIZE, IDX_DIM)),
        'page_table': page_table,
        'c_prompt': nrm(ks[10], (BATCH, D_MODEL)),
        'c_sample': nrm(ks[11], (DEC_BATCH, D_MODEL)),
        'rel_bias': nrm(ks[12], (REL_BUCKETS, REL_HEADS), 0.5),
        'norm_g': gain(ks[13], (DEPTH, D_MODEL)),
        'ada_w': nrm(ks[14], (DEPTH, D_MODEL, 3 * D_MODEL), 0.5 * D_MODEL ** -0.5),
        'ada_b': nrm(ks[15], (DEPTH, 3 * D_MODEL), 0.02),
        'w_in_even': nrm(ks[16], (N_EVEN, D_MODEL, even_in), D_MODEL ** -0.5),
        'ga_q': gain(ks[17], (N_EVEN, A_HEAD_DIM)),
        'ga_k': gain(ks[18], (N_EVEN, A_HEAD_DIM)),
        'gb_q': gain(ks[19], (N_EVEN, B_NOPE + B_ROPE)),
        'gb_k': gain(ks[20], (N_EVEN, B_NOPE + B_ROPE)),
        'g_ckv': gain(ks[21], (N_EVEN, B_KV_RANK)),
        'w_uk': nrm(ks[22], (N_EVEN, B_KV_RANK, B_HEADS, B_NOPE), B_KV_RANK ** -0.5),
        'w_uv': nrm(ks[23], (N_EVEN, B_KV_RANK, B_HEADS, B_VDIM), B_KV_RANK ** -0.5),
        'w_out_even': nrm(ks[24], (N_EVEN, A_WIDTH + B_WIDTH, D_MODEL), (A_WIDTH + B_WIDTH) ** -0.5),
        'w_in_odd': nrm(ks[25], (N_ODD, D_MODEL, odd_in), D_MODEL ** -0.5),
        'gc_q': gain(ks[26], (N_ODD, C_HEAD_DIM)),
        'gc_k': gain(ks[27], (N_ODD, C_HEAD_DIM)),
        'w_out_odd': nrm(ks[28], (N_ODD, C_WIDTH, D_MODEL), C_WIDTH ** -0.5),
    }


def reference(x_prompt, x_sample, cache_moba_k, cache_moba_v, cache_mla_ckv, cache_mla_kpe,
              cache_dsa_k, cache_dsa_v, cache_dsa_kidx, page_table, c_prompt, c_sample,
              rel_bias, norm_g, ada_w, ada_b, w_in_even, ga_q, ga_k, gb_q, gb_k, g_ckv,
              w_uk, w_uv, w_out_even, w_in_odd, gc_q, gc_k, w_out_odd):
    weights = (rel_bias, norm_g, ada_w, ada_b, w_in_even, ga_q, ga_k, gb_q, gb_k, g_ckv, w_uk, w_uv,
               w_out_even, w_in_odd, gc_q, gc_k, w_out_odd)
    y_prompt, ev_p, od_p = trunk(x_prompt, c_prompt, None, (None,) * 4, (None,) * 3, weights)
    y_sample, ev_s, od_s = trunk(x_sample, c_sample, page_table,
                                 (cache_moba_k, cache_moba_v, cache_mla_ckv, cache_mla_kpe),
                                 (cache_dsa_k, cache_dsa_v, cache_dsa_kidx), weights)
    moba_k_p = jnp.stack([r[0] for r in ev_p])
    moba_v_p = jnp.stack([r[1] for r in ev_p])
    mla_ckv_p = jnp.stack([r[2] for r in ev_p])
    mla_kpe_p = jnp.stack([r[3] for r in ev_p])
    dsa_k_p = jnp.stack([r[0] for r in od_p])
    dsa_v_p = jnp.stack([r[1] for r in od_p])
    dsa_kidx_p = jnp.stack([r[2] for r in od_p])
    moba_k_s = jnp.stack([r[0] for r in ev_s])
    moba_v_s = jnp.stack([r[1] for r in ev_s])
    mla_ckv_s = jnp.stack([r[2] for r in ev_s])
    mla_kpe_s = jnp.stack([r[3] for r in ev_s])
    dsa_k_s = jnp.stack([r[0] for r in od_s])
    dsa_v_s = jnp.stack([r[1] for r in od_s])
    dsa_kidx_s = jnp.stack([r[2] for r in od_s])
    return (y_prompt, y_sample, moba_k_p, moba_v_p, mla_ckv_p, mla_kpe_p, dsa_k_p, dsa_v_p, dsa_kidx_p,
            moba_k_s, moba_v_s, mla_ckv_s, mla_kpe_s, dsa_k_s, dsa_v_s, dsa_kidx_s)
```

```python
import functools
import math

import jax
import jax.numpy as jnp
import numpy as np
from jax import lax
from jax.experimental import pallas as pl
from jax.experimental.pallas import tpu as pltpu

F32 = jnp.float32
BF16 = jnp.bfloat16

D_MODEL = 1024
N_HEADS = 8
N_KV_HEADS = 2
GROUP = N_HEADS // N_KV_HEADS
A_HEAD_DIM = 64
B_NOPE = 64
B_ROPE = 32
B_QK = B_NOPE + B_ROPE
B_VDIM = 64
B_KV_RANK = 256
C_HEAD_DIM = 128
IDX_HEADS = 8
IDX_DIM = 64
MOBA_BLOCK = 256
MOBA_TOPK = 3
DSA_TOPK = 256
PAGE = 128
REL_BUCKETS = 32
REL_MAX_DIST = 128
ROPE_THETA = 10000.0
EPS = 1e-6
LANE = 128
NEG = -1e30
BISECT_ITERS = 40
VMEM_LIMIT = 56 * 1024 * 1024


def _bucket_thresholds():
    n = np.arange(0, 4 * REL_MAX_DIST)
    exact = REL_BUCKETS // 2
    nf = np.maximum(n, 1).astype(np.float64)
    big = exact + (np.log(nf / exact) / math.log(REL_MAX_DIST / exact) * (REL_BUCKETS - exact)).astype(np.int32)
    b = np.where(n < exact, n, np.minimum(big, REL_BUCKETS - 1))
    return [int(np.argmax(b >= k)) for k in range(1, REL_BUCKETS)]


BUCKET_THR = _bucket_thresholds()
FAR_DIST = BUCKET_THR[-1]


def _cparams(sem):
    return pltpu.CompilerParams(dimension_semantics=sem, vmem_limit_bytes=VMEM_LIMIT)


def _bias_kernel(rel_ref, o_ref, *, tq, ncols, offset):
    t = lax.broadcasted_iota(jnp.int32, (tq, ncols), 0)
    c = lax.broadcasted_iota(jnp.int32, (tq, ncols), 1)
    dist = jnp.maximum(t - c + offset, 0)
    masks = [dist >= th for th in BUCKET_THR]
    for h in range(N_HEADS):
        v = jnp.full((tq, ncols), rel_ref[0, h], F32)
        for k, mk in enumerate(masks):
            v = jnp.where(mk, rel_ref[k + 1, h], v)
        o_ref[h * tq:(h + 1) * tq, :] = v


def _bias_tile(rel_bias, tq, ncols, offset):
    return pl.pallas_call(
        functools.partial(_bias_kernel, tq=tq, ncols=ncols, offset=offset),
        out_shape=jax.ShapeDtypeStruct((N_HEADS * tq, ncols), F32),
        in_specs=[pl.BlockSpec(memory_space=pltpu.SMEM)],
        out_specs=pl.BlockSpec(memory_space=pltpu.VMEM),
        name="t5_bias_tile",
    )(rel_bias)


def _mod_kernel(c_ref, w_ref, b_ref, o_ref):
    c = c_ref[...]
    s = c * jax.nn.sigmoid(c)
    o_ref[0] = jnp.dot(s.astype(BF16), w_ref[0].astype(BF16), preferred_element_type=F32) + b_ref[0]


def _modulation(c_all, ada_w, ada_b):
    depth, _, n3 = ada_w.shape
    r = c_all.shape[0]
    tn = 1024
    return pl.pallas_call(
        _mod_kernel,
        out_shape=jax.ShapeDtypeStruct((depth, r, n3), F32),
        grid=(depth, n3 // tn),
        in_specs=[pl.BlockSpec((r, D_MODEL), lambda l, j: (0, 0)),
                  pl.BlockSpec((1, D_MODEL, tn), lambda l, j: (l, 0, j)),
                  pl.BlockSpec((1, 1, tn), lambda l, j: (l, 0, j))],
        out_specs=pl.BlockSpec((1, r, tn), lambda l, j: (l, 0, j)),
        compiler_params=_cparams(("parallel", "parallel")),
        name="adaln_modulation",
    )(c_all, ada_w, ada_b.reshape(depth, 1, n3))


def _modulated_norm(x, mod, norm_g):
    xn = x * lax.rsqrt(jnp.mean(x * x, axis=-1, keepdims=True) + EPS)
    shift = mod[:, 0:D_MODEL]
    scale = mod[:, D_MODEL:2 * D_MODEL]
    return (xn * norm_g) * (1.0 + scale) + shift


def _silu(x):
    return x * jax.nn.sigmoid(x)


def _block_rms(xh, inv_n):
    ssq = jnp.sum(xh * xh, axis=-1, keepdims=True)
    return xh * lax.rsqrt(ssq * inv_n + EPS)


def _rope_block(xh, c, s1, s2):
    return xh * c + pltpu.roll(xh, LANE - 16, 1) * s1 + pltpu.roll(xh, 16, 1) * s2


def _even_proj_kernel(x_ref, mod_ref, ng_ref, w_ref, wuk_ref, wuv_ref, gaq_ref, gak_ref, gbq_ref, gbk_ref,
                      gckv_ref, rc_ref, rs1_ref, rs2_ref,
                      qa_ref, ka_ref, va_ref, sga_ref, qb_ref, ckv_ref, kpe_ref, kb_ref, vb_ref, sgb_ref):
    h = _modulated_norm(x_ref[...], mod_ref[0], ng_ref[...])
    z = jnp.dot(h.astype(BF16), w_ref[...], preferred_element_type=F32)
    o_qa, o_kv, o_ga, o_qb, o_ckv, o_kpe, o_gb = 0, 1024, 1280, 2304, 3328, 3584, 3712
    rc, rs1, rs2 = rc_ref[0], rs1_ref[0], rs2_ref[0]
    for hd in range(N_HEADS):
        blk = slice(hd * LANE, (hd + 1) * LANE)
        xq = z[:, o_qa + hd * LANE:o_qa + (hd + 1) * LANE]
        qa_ref[:, blk] = (_block_rms(xq, 1.0 / A_HEAD_DIM) * gaq_ref[:, blk] * (A_HEAD_DIM ** -0.5)).astype(qa_ref.dtype)
    ka = z[:, o_kv:o_kv + LANE]
    lane = lax.broadcasted_iota(jnp.int32, ka.shape, 1)
    lo = lane < A_HEAD_DIM
    k2 = ka * ka
    s0 = jnp.sum(jnp.where(lo, k2, 0.0), axis=-1, keepdims=True)
    s1 = jnp.sum(jnp.where(lo, 0.0, k2), axis=-1, keepdims=True)
    rk = lax.rsqrt(jnp.where(lo, s0, s1) * (1.0 / A_HEAD_DIM) + EPS)
    ka_ref[...] = ka * rk * gak_ref[...]
    va_ref[...] = z[:, o_kv + LANE:o_kv + 2 * LANE]
    sga_ref[...] = _silu(z[:, o_ga:o_ga + 1024]).astype(sga_ref.dtype)
    ckv = z[:, o_ckv:o_ckv + B_KV_RANK]
    ckv_n = ckv * lax.rsqrt(jnp.mean(ckv * ckv, axis=-1, keepdims=True) + EPS) * gckv_ref[...]
    ckv_ref[...] = ckv_n
    kpe_blk = z[:, o_kpe:o_kpe + LANE]
    kpe_ref[...] = kpe_blk[:, B_NOPE:B_NOPE + B_ROPE]
    ckv_b = ckv_n.astype(BF16)
    kn = jnp.dot(ckv_b, wuk_ref[...], preferred_element_type=F32)
    vb_ref[...] = jnp.dot(ckv_b, wuv_ref[...], preferred_element_type=F32).astype(vb_ref.dtype)
    for hd in range(N_HEADS):
        blk = slice(hd * LANE, (hd + 1) * LANE)
        xq = z[:, o_qb + hd * LANE:o_qb + (hd + 1) * LANE]
        qn = _block_rms(xq, 1.0 / B_QK) * gbq_ref[...]
        qb_ref[:, blk] = (_rope_block(qn, rc, rs1, rs2) * (B_QK ** -0.5)).astype(qb_ref.dtype)
        xk = kn[:, blk] + kpe_blk
        kk = _block_rms(xk, 1.0 / B_QK) * gbk_ref[...]
        kb_ref[:, blk] = _rope_block(kk, rc, rs1, rs2).astype(kb_ref.dtype)
    sgb_ref[...] = _silu(z[:, o_gb:o_gb + 1024]).astype(sgb_ref.dtype)


def _pad_heads(w, n_heads, width, offset=0):
    k = w.shape[0]
    w = w.reshape(k, n_heads, width)
    out = jnp.zeros((k, n_heads, LANE), w.dtype)
    out = out.at[:, :, offset:offset + width].set(w)
    return out.reshape(k, n_heads * LANE)


def _pad_kv_grouped(w):
    k = w.shape[0]
    w = w.reshape(k, N_KV_HEADS, GROUP, A_HEAD_DIM)
    out = jnp.zeros((k, N_KV_HEADS, GROUP, LANE), w.dtype)
    out = out.at[:, 0, :, 0:A_HEAD_DIM].set(w[:, 0])
    out = out.at[:, 1, :, A_HEAD_DIM:].set(w[:, 1])
    return out.reshape(k, N_HEADS * LANE)


def _rope_tables(positions):
    half = B_ROPE // 2
    inv = ROPE_THETA ** (-np.arange(half, dtype=np.float64) / half)
    ang = np.asarray(positions, dtype=np.float64)[:, None] * inv[None, :]
    cos, sin = np.cos(ang), np.sin(ang)
    n = ang.shape[0]
    c = np.ones((n, LANE)); s1 = np.zeros((n, LANE)); s2 = np.zeros((n, LANE))
    c[:, B_NOPE:B_NOPE + half] = cos
    c[:, B_NOPE + half:B_NOPE + B_ROPE] = cos
    s1[:, B_NOPE:B_NOPE + half] = -sin
    s2[:, B_NOPE + half:B_NOPE + B_ROPE] = sin
    return (jnp.asarray(c, F32), jnp.asarray(s1, F32), jnp.asarray(s2, F32))


def _even_weights(w_in, ga_q, ga_k, gb_q, gb_k, g_ckv, w_uk, w_uv):
    k = D_MODEL
    o = np.cumsum((0, 512, 128, 128, 512, 512, 256, 256, 32, 512))
    seg = [w_in[:, o[i]:o[i + 1]] for i in range(9)]
    w_qa, w_ka, w_va, w_ga, w_qbn, w_qbp, w_ckv, w_kpe, w_gb = seg
    qb = jnp.zeros((k, N_HEADS, LANE), w_in.dtype)
    qb = qb.at[:, :, 0:B_NOPE].set(w_qbn.reshape(k, N_HEADS, B_NOPE))
    qb = qb.at[:, :, B_NOPE:B_QK].set(w_qbp.reshape(k, N_HEADS, B_ROPE))
    kpe = jnp.zeros((k, LANE), w_in.dtype).at[:, B_NOPE:B_QK].set(w_kpe)
    w = jnp.concatenate([_pad_kv_grouped(w_qa), w_ka, w_va, _pad_kv_grouped(w_ga), qb.reshape(k, -1), w_ckv, kpe,
                         _pad_heads(w_gb, N_HEADS, B_VDIM)], axis=1).astype(BF16)
    wuk = _pad_heads(w_uk.reshape(B_KV_RANK, -1), N_HEADS, B_NOPE).astype(BF16)
    wuv = _pad_heads(w_uv.reshape(B_KV_RANK, -1), N_HEADS, B_VDIM).astype(BF16)
    gaq = _pad_kv_grouped(jnp.tile(ga_q, N_HEADS)[None, :])
    gak = jnp.tile(ga_k, N_KV_HEADS)[None, :]
    gbq = jnp.zeros((1, LANE), F32).at[0, 0:B_QK].set(gb_q)
    gbk = jnp.zeros((1, LANE), F32).at[0, 0:B_QK].set(gb_k)
    return w, wuk, wuv, gaq, gak, gbq, gbk, g_ckv[None, :]


def _token_tile(n):
    return 256 if n % 256 == 0 else n


def _even_proj(x, mod, norm_g, wts, rope, tiles_per_group):
    w, wuk, wuv, gaq, gak, gbq, gbk, gckv = wts
    n = x.shape[0]
    tm = rope[0].shape[1]
    ntab = rope[0].shape[0]
    r = mod.shape[1]
    ne = w.shape[1]
    row = lambda i: (i, 0)
    const = lambda i: (0, 0)
    tab = lambda i: (i % ntab, 0, 0)
    outs = [((n, 1024), BF16), ((n, LANE), F32), ((n, LANE), F32), ((n, 1024), BF16), ((n, 1024), BF16),
            ((n, B_KV_RANK), F32), ((n, B_ROPE), F32), ((n, 1024), BF16), ((n, 1024), BF16), ((n, 1024), BF16)]
    return pl.pallas_call(
        _even_proj_kernel,
        out_shape=[jax.ShapeDtypeStruct(s, d) for s, d in outs],
        grid=(n // tm,),
        in_specs=[pl.BlockSpec((tm, D_MODEL), row),
                  pl.BlockSpec((1, r, 3 * D_MODEL), lambda i: (i // tiles_per_group, 0, 0)),
                  pl.BlockSpec((1, D_MODEL), const),
                  pl.BlockSpec((D_MODEL, ne), const),
                  pl.BlockSpec((B_KV_RANK, 1024), const),
                  pl.BlockSpec((B_KV_RANK, 1024), const),
                  pl.BlockSpec((1, 1024), const),
                  pl.BlockSpec((1, LANE), const),
                  pl.BlockSpec((1, LANE), const),
                  pl.BlockSpec((1, LANE), const),
                  pl.BlockSpec((1, B_KV_RANK), const),
                  pl.BlockSpec((1, tm, LANE), tab),
                  pl.BlockSpec((1, tm, LANE), tab),
                  pl.BlockSpec((1, tm, LANE), tab)],
        out_specs=[pl.BlockSpec((tm, s[1]), row) for s, _ in outs],
        compiler_params=_cparams(("parallel",)),
        name="even_proj",
    )(x, mod, norm_g, w, wuk, wuv, gaq, gak, gbq, gbk, gckv, *rope)


def _odd_proj_kernel(x_ref, mod_ref, ng_ref, w_ref, gcq_ref, gck_ref,
                     qc_ref, kc_ref, vc_ref, sgc_ref, qi_ref, ki_ref, kib_ref, wi_ref, kcb_ref, vcb_ref):
    h = _modulated_norm(x_ref[...], mod_ref[0], ng_ref[...])
    z = jnp.dot(h.astype(BF16), w_ref[...], preferred_element_type=F32)
    o_qc, o_kc, o_vc, o_gc, o_qi, o_ki, o_wi = 0, 1024, 1280, 1536, 2560, 3584, 3712
    for hd in range(N_HEADS):
        blk = slice(hd * LANE, (hd + 1) * LANE)
        xq = z[:, o_qc + hd * LANE:o_qc + (hd + 1) * LANE]
        qc_ref[:, blk] = (_block_rms(xq, 1.0 / C_HEAD_DIM) * gcq_ref[...] * (C_HEAD_DIM ** -0.5)).astype(qc_ref.dtype)
    for kv in range(N_KV_HEADS):
        blk = slice(kv * LANE, (kv + 1) * LANE)
        kn = _block_rms(z[:, o_kc + kv * LANE:o_kc + (kv + 1) * LANE], 1.0 / C_HEAD_DIM) * gck_ref[...]
        kc_ref[:, blk] = kn
        kcb_ref[:, blk] = kn.astype(kcb_ref.dtype)
    vc = z[:, o_vc:o_vc + 2 * LANE]
    vc_ref[...] = vc
    vcb_ref[...] = vc.astype(vcb_ref.dtype)
    sgc_ref[...] = _silu(z[:, o_gc:o_gc + 1024]).astype(sgc_ref.dtype)
    qi_ref[...] = (z[:, o_qi:o_qi + 1024] * (IDX_DIM ** -0.5)).astype(qi_ref.dtype)
    ki_blk = z[:, o_ki:o_ki + LANE]
    ki_ref[...] = ki_blk[:, 0:IDX_DIM]
    kib_ref[...] = ki_blk.astype(kib_ref.dtype)
    wi_ref[...] = z[:, o_wi:o_wi + LANE] * (IDX_HEADS ** -0.5)


def _odd_weights(w_in, gc_q, gc_k):
    k = D_MODEL
    o = np.cumsum((0, 1024, 256, 256, 1024, 512, 64, 8))
    w_qc, w_kc, w_vc, w_gc, w_qi, w_ki, w_wi = [w_in[:, o[i]:o[i + 1]] for i in range(7)]
    ki = jnp.zeros((k, LANE), w_in.dtype).at[:, 0:IDX_DIM].set(w_ki)
    wi = jnp.zeros((k, LANE), w_in.dtype).at[:, 0:IDX_HEADS].set(w_wi)
    w = jnp.concatenate([w_qc, w_kc, w_vc, w_gc, _pad_heads(w_qi, IDX_HEADS, IDX_DIM), ki, wi], axis=1).astype(BF16)
    return w, gc_q[None, :], gc_k[None, :]


def _odd_proj(x, mod, norm_g, wts, tm, tiles_per_group):
    w, gcq, gck = wts
    n = x.shape[0]
    r = mod.shape[1]
    no = w.shape[1]
    row = lambda i: (i, 0)
    const = lambda i: (0, 0)
    outs = [((n, 1024), BF16), ((n, 2 * LANE), F32), ((n, 2 * LANE), F32), ((n, 1024), BF16), ((n, 1024), BF16),
            ((n, IDX_DIM), F32), ((n, LANE), BF16), ((n, LANE), F32), ((n, 2 * LANE), BF16), ((n, 2 * LANE), BF16)]
    return pl.pallas_call(
        _odd_proj_kernel,
        out_shape=[jax.ShapeDtypeStruct(s, d) for s, d in outs],
        grid=(n // tm,),
        in_specs=[pl.BlockSpec((tm, D_MODEL), row),
                  pl.BlockSpec((1, r, 3 * D_MODEL), lambda i: (i // tiles_per_group, 0, 0)),
                  pl.BlockSpec((1, D_MODEL), const),
                  pl.BlockSpec((D_MODEL, no), const),
                  pl.BlockSpec((1, LANE), const),
                  pl.BlockSpec((1, LANE), const)],
        out_specs=[pl.BlockSpec((tm, s[1]), row) for s, _ in outs],
        compiler_params=_cparams(("parallel",)),
        name="odd_proj",
    )(x, mod, norm_g, w, gcq, gck)


def _out_proj_kernel(*refs, n_terms):
    x_ref, mod_ref = refs[0], refs[1]
    o_ref = refs[-1]
    acc = None
    for i in range(n_terms):
        a_ref, g_ref, w_ref = refs[2 + 3 * i:5 + 3 * i]
        m = (a_ref[...].astype(F32) * g_ref[...].astype(F32)).astype(BF16)
        d = jnp.dot(m, w_ref[...], preferred_element_type=F32)
        acc = d if acc is None else acc + d
    gate = mod_ref[0][:, 2 * D_MODEL:3 * D_MODEL]
    o_ref[...] = x_ref[...] + gate * acc


def _out_proj(x, mod, terms, tm, tiles_per_group):
    n = x.shape[0]
    r = mod.shape[1]
    row = lambda i: (i, 0)
    const = lambda i: (0, 0)
    in_specs = [pl.BlockSpec((tm, D_MODEL), row),
                pl.BlockSpec((1, r, 3 * D_MODEL), lambda i: (i // tiles_per_group, 0, 0))]
    args = [x, mod]
    for a, g, w in terms:
        in_specs += [pl.BlockSpec((tm, a.shape[1]), row), pl.BlockSpec((tm, g.shape[1]), row),
                     pl.BlockSpec(w.shape, const)]
        args += [a, g, w]
    return pl.pallas_call(
        functools.partial(_out_proj_kernel, n_terms=len(terms)),
        out_shape=jax.ShapeDtypeStruct((n, D_MODEL), F32),
        grid=(n // tm,),
        in_specs=in_specs,
        out_specs=pl.BlockSpec((tm, D_MODEL), row),
        compiler_params=_cparams(("parallel",)),
        name="out_proj",
    )(*args)


def _kmean_kernel(k_ref, o_ref):
    k = k_ref[0]
    nb = k.shape[0] // MOBA_BLOCK
    o_ref[0] = jnp.sum(k.reshape(nb, MOBA_BLOCK, LANE), axis=1) * (1.0 / MOBA_BLOCK)


def _block_means(k):
    b, l, _ = k.shape
    nb = l // MOBA_BLOCK
    return pl.pallas_call(
        _kmean_kernel,
        out_shape=jax.ShapeDtypeStruct((b, nb, LANE), F32),
        grid=(b,),
        in_specs=[pl.BlockSpec((1, l, LANE), lambda i: (i, 0, 0))],
        out_specs=pl.BlockSpec((1, nb, LANE), lambda i: (i, 0, 0)),
        compiler_params=_cparams(("parallel",)),
        name="moba_block_means",
    )(k)


_NT = (((1,), (1,)), ((), ()))


def _top_blocks(gate, n_valid, n_pick):
    nb = gate.shape[1]
    lane = lax.broadcasted_iota(jnp.int32, gate.shape, 1)
    g = jnp.where(lane < n_valid, gate, NEG)
    sel = jnp.zeros(gate.shape, F32)
    for r in range(n_pick):
        mx = jnp.max(g, axis=-1, keepdims=True)
        idx = jnp.min(jnp.where(g == mx, lane, nb), axis=-1, keepdims=True)
        hit = lane == idx
        sel = jnp.where(jnp.logical_and(hit, n_valid > r), 1.0, sel)
        g = jnp.where(hit, NEG, g)
    return sel


def _softmax_step(s, v, m_ref, l_ref, acc_ref):
    m_prev = m_ref[...]
    m_new = jnp.maximum(m_prev, jnp.max(s, axis=-1, keepdims=True))
    alpha = jnp.exp(m_prev - m_new)
    p = jnp.exp(s - m_new)
    l_ref[...] = alpha * l_ref[...] + jnp.sum(p, axis=-1, keepdims=True)
    acc_ref[...] = alpha * acc_ref[...] + jnp.dot(p.astype(BF16), v, preferred_element_type=F32)
    m_ref[...] = m_new


def _moba_prompt_kernel(ii_ref, jj_ref, last_ref, q_ref, k_ref, v_ref, km_ref, bd_ref, bs_ref, bf_ref, o_ref,
                        qs_ref, sel_ref, m_ref, l_ref, acc_ref):
    step = pl.program_id(1)
    i = ii_ref[step]
    j = jj_ref[step]
    tq = MOBA_BLOCK
    k = k_ref[0].astype(BF16)
    v = v_ref[0].astype(BF16)

    @pl.when(j == i)
    def _own():
        for h in range(N_HEADS):
            qs_ref[h * tq:(h + 1) * tq, :] = q_ref[0, :, h * LANE:(h + 1) * LANE]
        gate = lax.dot_general(qs_ref[...], km_ref[0].astype(BF16), _NT, preferred_element_type=F32)
        sel_ref[...] = _top_blocks(gate, i, MOBA_TOPK)
        s = lax.dot_general(qs_ref[...], k, _NT, preferred_element_type=F32) + bd_ref[...]
        s3 = s.reshape(N_HEADS, tq, tq)
        row = lax.broadcasted_iota(jnp.int32, (tq, tq), 0)
        col = lax.broadcasted_iota(jnp.int32, (tq, tq), 1)
        s = jnp.where((col <= row)[None], s3, NEG).reshape(N_HEADS * tq, tq)
        m = jnp.max(s, axis=-1, keepdims=True)
        p = jnp.exp(s - m)
        m_ref[...] = m
        l_ref[...] = jnp.sum(p, axis=-1, keepdims=True)
        acc_ref[...] = jnp.dot(p.astype(BF16), v, preferred_element_type=F32)

    @pl.when(j != i)
    def _past():
        sel = sel_ref[...]
        lane = lax.broadcasted_iota(jnp.int32, sel.shape, 1)
        chosen = jnp.sum(jnp.where(lane == j, sel, 0.0), axis=-1, keepdims=True) > 0.0
        s = lax.dot_general(qs_ref[...], k, _NT, preferred_element_type=F32)
        bias = jnp.where(j == i - 1, bs_ref[...], bf_ref[...])
        s = jnp.where(chosen, s + bias, NEG)
        _softmax_step(s, v, m_ref, l_ref, acc_ref)

    @pl.when(last_ref[step] == 1)
    def _fin():
        out = acc_ref[...] / l_ref[...]
        for h in range(N_HEADS):
            o_ref[0, :, h * LANE:(h + 1) * LANE] = out[h * tq:(h + 1) * tq, :].astype(o_ref.dtype)


def _causal_pairs(nq, own_first):
    ii, jj, last = [], [], []
    for i in range(nq):
        js = ([i] + list(range(i))) if own_first else list(range(i + 1))
        for n, j in enumerate(js):
            ii.append(i); jj.append(j); last.append(1 if n == len(js) - 1 else 0)
    return (jnp.asarray(ii, jnp.int32), jnp.asarray(jj, jnp.int32), jnp.asarray(last, jnp.int32))


def _moba_prompt(qa, ka, va, kmean, bias_diag, bias_sub, bias_far):
    b, t, _ = qa.shape
    tq = MOBA_BLOCK
    nq = t // tq
    nb = kmean.shape[1]
    ii, jj, last = _causal_pairs(nq, own_first=True)
    r = N_HEADS * tq
    grid_spec = pltpu.PrefetchScalarGridSpec(
        num_scalar_prefetch=3,
        grid=(b, int(ii.shape[0])),
        in_specs=[pl.BlockSpec((1, tq, 1024), lambda bb, s, ii, jj, la: (bb, ii[s], 0)),
                  pl.BlockSpec((1, tq, LANE), lambda bb, s, ii, jj, la: (bb, jj[s], 0)),
                  pl.BlockSpec((1, tq, LANE), lambda bb, s, ii, jj, la: (bb, jj[s], 0)),
                  pl.BlockSpec((1, nb, LANE), lambda bb, s, ii, jj, la: (bb, 0, 0)),
                  pl.BlockSpec((r, tq), lambda bb, s, ii, jj, la: (0, 0)),
                  pl.BlockSpec((r, tq), lambda bb, s, ii, jj, la: (0, 0)),
                  pl.BlockSpec((r, tq), lambda bb, s, ii, jj, la: (0, 0))],
        out_specs=pl.BlockSpec((1, tq, 1024), lambda bb, s, ii, jj, la: (bb, ii[s], 0)),
        scratch_shapes=[pltpu.VMEM((r, LANE), BF16), pltpu.VMEM((r, nb), F32), pltpu.VMEM((r, 1), F32),
                        pltpu.VMEM((r, 1), F32), pltpu.VMEM((r, LANE), F32)])
    return pl.pallas_call(
        _moba_prompt_kernel,
        out_shape=jax.ShapeDtypeStruct((b, t, 1024), BF16),
        grid_spec=grid_spec,
        compiler_params=_cparams(("parallel", "arbitrary")),
        name="moba_prompt",
    )(ii, jj, last, qa, ka, va, kmean, bias_diag, bias_sub, bias_far)


def _mla_prompt_kernel(ii_ref, jj_ref, last_ref, q_ref, k_ref, v_ref, o_ref, m_ref, l_ref, acc_ref):
    step = pl.program_id(1)
    i = ii_ref[step]
    j = jj_ref[step]
    tq = q_ref.shape[1]

    @pl.when(j == 0)
    def _init():
        m_ref[...] = jnp.full(m_ref.shape, NEG, F32)
        l_ref[...] = jnp.zeros(l_ref.shape, F32)
        acc_ref[...] = jnp.zeros(acc_ref.shape, F32)

    row = lax.broadcasted_iota(jnp.int32, (tq, tq), 0)
    col = lax.broadcasted_iota(jnp.int32, (tq, tq), 1)
    visible = jnp.logical_or(col <= row, j < i)
    for h in range(N_HEADS):
        blk = slice(h * LANE, (h + 1) * LANE)
        s = lax.dot_general(q_ref[0, :, blk], k_ref[0, :, blk], _NT, preferred_element_type=F32)
        s = jnp.where(visible, s, NEG)
        _softmax_step(s, v_ref[0, :, blk], m_ref.at[h], l_ref.at[h], acc_ref.at[:, blk])

    @pl.when(last_ref[step] == 1)
    def _fin():
        for h in range(N_HEADS):
            blk = slice(h * LANE, (h + 1) * LANE)
            o_ref[0, :, blk] = (acc_ref[:, blk] / l_ref[h]).astype(o_ref.dtype)


def _mla_prompt(qb, kb, vb):
    b, t, _ = qb.shape
    tq = 512 if t % 512 == 0 else 256
    nq = t // tq
    ii, jj, last = _causal_pairs(nq, own_first=False)
    grid_spec = pltpu.PrefetchScalarGridSpec(
        num_scalar_prefetch=3,
        grid=(b, int(ii.shape[0])),
        in_specs=[pl.BlockSpec((1, tq, 1024), lambda bb, s, ii, jj, la: (bb, ii[s], 0)),
                  pl.BlockSpec((1, tq, 1024), lambda bb, s, ii, jj, la: (bb, jj[s], 0)),
                  pl.BlockSpec((1, tq, 1024), lambda bb, s, ii, jj, la: (bb, jj[s], 0))],
        out_specs=pl.BlockSpec((1, tq, 1024), lambda bb, s, ii, jj, la: (bb, ii[s], 0)),
        scratch_shapes=[pltpu.VMEM((N_HEADS, tq, 1), F32), pltpu.VMEM((N_HEADS, tq, 1), F32),
                        pltpu.VMEM((tq, 1024), F32)])
    return pl.pallas_call(
        _mla_prompt_kernel,
        out_shape=jax.ShapeDtypeStruct((b, t, 1024), BF16),
        grid_spec=grid_spec,
        compiler_params=_cparams(("parallel", "arbitrary")),
        name="mla_prompt",
    )(ii, jj, last, qb, kb, vb)


def _kth_largest_threshold(read_chunk, n_chunks, rows, topk):
    big = -NEG

    def stat(c, carry):
        mx, mn = carry
        x = read_chunk(c)
        mx = jnp.maximum(mx, jnp.max(x, axis=-1, keepdims=True))
        mn = jnp.minimum(mn, jnp.min(jnp.where(x > 0.5 * NEG, x, big), axis=-1, keepdims=True))
        return mx, mn

    mx, mn = lax.fori_loop(0, n_chunks, stat, (jnp.full((rows, 1), NEG, F32), jnp.full((rows, 1), big, F32)))
    lo0 = mn
    hi0 = mx + (mx - mn) + 1.0

    def bisect(_, carry):
        lo, hi = carry
        mid = lo + (hi - lo) * 0.5

        def count(c, acc):
            return acc + jnp.sum(jnp.where(read_chunk(c) >= mid, 1.0, 0.0), axis=-1, keepdims=True)

        cnt = lax.fori_loop(0, n_chunks, count, jnp.zeros((rows, 1), F32))
        ge = cnt >= float(topk)
        return jnp.where(ge, mid, lo), jnp.where(ge, hi, mid)

    lo, _ = lax.fori_loop(0, BISECT_ITERS, bisect, (lo0, hi0))
    return lo


def _dsa_prompt_kernel(q_ref, qi_ref, wi_ref, kib_ref, kc_ref, vc_ref, bd_ref, bs_ref, bf_ref, o_ref,
                       qs_ref, qis_ref, wis_ref, s_ref, m_ref, l_ref, acc_ref, *, topk, chunk):
    i = pl.program_id(1)
    tq = q_ref.shape[1]
    half = GROUP * tq
    wi = wi_ref[0]
    for h in range(N_HEADS):
        blk = slice(h * LANE, (h + 1) * LANE)
        qs_ref[h * tq:(h + 1) * tq, :] = q_ref[0, :, blk]
        qis_ref[h * tq:(h + 1) * tq, :] = qi_ref[0, :, blk]
        wis_ref[h * tq:(h + 1) * tq, :] = wi[:, h:h + 1]
    s_ref[...] = jnp.full(s_ref.shape, NEG, F32)
    m_ref[...] = jnp.full(m_ref.shape, NEG, F32)
    l_ref[...] = jnp.zeros(l_ref.shape, F32)
    acc_ref[...] = jnp.zeros(acc_ref.shape, F32)

    def keys(j):
        return pl.ds(pl.multiple_of(j * tq, tq), tq)

    def index_tile(j):
        sc = lax.dot_general(qis_ref[...], kib_ref[0, keys(j), :], _NT, preferred_element_type=F32)
        sc = jnp.maximum(sc, 0.0) * wis_ref[...]
        return jnp.sum(sc.reshape(N_HEADS, tq, tq), axis=0)

    def index_body(j, c):
        s_ref[:, keys(j)] = index_tile(j)
        return c

    lax.fori_loop(0, i, index_body, 0)
    row = lax.broadcasted_iota(jnp.int32, (tq, tq), 0)
    col = lax.broadcasted_iota(jnp.int32, (tq, tq), 1)
    s_ref[:, keys(i)] = jnp.where(col <= row, index_tile(i), NEG)

    n_chunks = ((i + 1) * tq + chunk - 1) // chunk
    thr = _kth_largest_threshold(lambda c: s_ref[:, pl.ds(pl.multiple_of(c * chunk, chunk), chunk)], n_chunks, tq, topk)

    def attend(j, bias_ref):
        keep = (s_ref[:, keys(j)] >= thr)[None]
        kt = kc_ref[0, keys(j), :]
        vt = vc_ref[0, keys(j), :]
        for kv in range(N_KV_HEADS):
            rows = slice(kv * half, (kv + 1) * half)
            lanes = slice(kv * LANE, (kv + 1) * LANE)
            s = lax.dot_general(qs_ref[rows, :], kt[:, lanes], _NT, preferred_element_type=F32) + bias_ref[rows, :]
            s = jnp.where(keep, s.reshape(GROUP, tq, tq), NEG).reshape(half, tq)
            _softmax_step(s, vt[:, lanes], m_ref.at[rows, :], l_ref.at[rows, :], acc_ref.at[rows, :])

    def far_body(j, c):
        attend(j, bf_ref)
        return c

    lax.fori_loop(0, jnp.maximum(i - 1, 0), far_body, 0)

    @pl.when(i >= 1)
    def _sub():
        attend(i - 1, bs_ref)

    attend(i, bd_ref)
    out = acc_ref[...] / l_ref[...]
    for h in range(N_HEADS):
        o_ref[0, :, h * LANE:(h + 1) * LANE] = out[h * tq:(h + 1) * tq, :].astype(o_ref.dtype)


def _dsa_prompt(qc, qi, wi, kib, kcb, vcb, bias_diag, bias_sub, bias_far):
    b, t, _ = qc.shape
    tq = MOBA_BLOCK
    nq = t // tq
    r = N_HEADS * tq
    topk = min(DSA_TOPK, t // 4)
    chunk = min(1024, t)
    tile = lambda bb, i: (bb, i, 0)
    seq = lambda bb, i: (bb, 0, 0)
    const = lambda bb, i: (0, 0)
    return pl.pallas_call(
        functools.partial(_dsa_prompt_kernel, topk=topk, chunk=chunk),
        out_shape=jax.ShapeDtypeStruct((b, t, 1024), BF16),
        grid=(b, nq),
        in_specs=[pl.BlockSpec((1, tq, 1024), tile), pl.BlockSpec((1, tq, 1024), tile),
                  pl.BlockSpec((1, tq, LANE), tile),
                  pl.BlockSpec((1, t, LANE), seq), pl.BlockSpec((1, t, 2 * LANE), seq),
                  pl.BlockSpec((1, t, 2 * LANE), seq),
                  pl.BlockSpec((r, tq), const), pl.BlockSpec((r, tq), const), pl.BlockSpec((r, tq), const)],
        out_specs=pl.BlockSpec((1, tq, 1024), tile),
        scratch_shapes=[pltpu.VMEM((r, LANE), BF16), pltpu.VMEM((r, LANE), BF16), pltpu.VMEM((r, 1), F32),
                        pltpu.VMEM((tq, t), F32), pltpu.VMEM((r, 1), F32), pltpu.VMEM((r, 1), F32),
                        pltpu.VMEM((r, LANE), F32)],
        compiler_params=_cparams(("parallel", "arbitrary")),
        name="dsa_prompt",
    )(qc, qi, wi, kib, kcb, vcb, bias_diag, bias_sub, bias_far)


def _out_weights_even(w_out):
    wa = _pad_kv_grouped(w_out[0:N_HEADS * A_HEAD_DIM].T).T.astype(BF16)
    wb = _pad_heads(w_out[N_HEADS * A_HEAD_DIM:].T, N_HEADS, B_VDIM).T.astype(BF16)
    return wa, wb


def _prompt_trunk(x, mods, rel_tiles, norm_g, even_w, out_even_w, odd_w, out_odd_w):
    b, t, _ = x.shape
    n = b * t
    tm = _token_tile(n)
    tpg = t // tm
    x2 = x.reshape(n, D_MODEL)
    bias_diag, bias_sub, bias_far = rel_tiles
    rope = tuple(tab.reshape(t // tm, tm, LANE) for tab in _rope_tables(np.arange(t)))
    mod0 = mods[0][:b].reshape(b, 1, 3 * D_MODEL)
    mod1 = mods[1][:b].reshape(b, 1, 3 * D_MODEL)

    qa, ka, va, sga, qb, ckv, kpe, kb, vb, sgb = _even_proj(x2, mod0, norm_g[0:1], even_w, rope, tpg)
    seq = lambda a: a.reshape(b, t, a.shape[-1])
    kmean = _block_means(seq(ka))
    oa = _moba_prompt(seq(qa), seq(ka), seq(va), kmean, bias_diag, bias_sub, bias_far)
    ob = _mla_prompt(seq(qb), seq(kb), seq(vb))
    wa, wb = out_even_w
    x2 = _out_proj(x2, mod0, [(oa.reshape(n, -1), sga, wa), (ob.reshape(n, -1), sgb, wb)], tm, tpg)

    qc, kc, vc, sgc, qi, ki, kib, wi, kcb, vcb = _odd_proj(x2, mod1, norm_g[1:2], odd_w, tm, tpg)
    oc = _dsa_prompt(seq(qc), seq(qi), seq(wi), seq(kib), seq(kcb), seq(vcb), bias_diag, bias_sub, bias_far)
    x2 = _out_proj(x2, mod1, [(oc.reshape(n, -1), sgc, out_odd_w)], tm, tpg)

    rows = (ka.reshape(1, b, t, N_KV_HEADS, A_HEAD_DIM), va.reshape(1, b, t, N_KV_HEADS, A_HEAD_DIM),
            ckv.reshape(1, b, t, B_KV_RANK), kpe.reshape(1, b, t, B_ROPE),
            kc.reshape(1, b, t, N_KV_HEADS, C_HEAD_DIM), vc.reshape(1, b, t, N_KV_HEADS, C_HEAD_DIM),
            ki.reshape(1, b, t, IDX_DIM))
    return x2.reshape(b, t, D_MODEL), rows


def _pages_copy(pt_ref, seq, first_page, n_pages, hbm_ref, buf_ref, slot, sem, start):
    def body(p, carry):
        cp = pltpu.make_async_copy(hbm_ref.at[pt_ref[seq, first_page + p]], buf_ref.at[slot, p], sem)
        if start:
            cp.start()
        else:
            cp.wait()
        return carry

    lax.fori_loop(0, n_pages, body, 0)


def _stream_chunks(pt_ref, chunk_pages, streams):
    b, c = pl.program_id(0), pl.program_id(1)
    nseq, nch = pl.num_programs(0), pl.num_programs(1)
    step = b * nch + c
    slot = step % 2

    def issue(seq, chunk, sl, start):
        for hbm_ref, buf_ref, sem_ref in streams:
            _pages_copy(pt_ref, seq, chunk * chunk_pages, chunk_pages, hbm_ref, buf_ref, sl, sem_ref.at[sl], start)

    @pl.when(step == 0)
    def _first():
        issue(0, 0, 0, True)

    @pl.when(step + 1 < nseq * nch)
    def _next():
        wrap = c + 1 == nch
        issue(jnp.where(wrap, b + 1, b), jnp.where(wrap, 0, c + 1), 1 - slot, True)

    issue(b, c, slot, False)
    return slot


def _online_update(s, v, m, l, acc):
    m_new = jnp.maximum(m, jnp.max(s, axis=-1, keepdims=True))
    alpha = jnp.exp(m - m_new)
    p = jnp.exp(s - m_new)
    l = alpha * l + jnp.sum(p, axis=-1, keepdims=True)
    acc = alpha * acc + jnp.dot(p.astype(BF16), v, preferred_element_type=F32)
    return m_new, l, acc


def _new_key_mask(rows, cols, n_new):
    t = lax.broadcasted_iota(jnp.int32, (rows, cols), 0) % 8
    c = lax.broadcasted_iota(jnp.int32, (rows, cols), 1)
    return jnp.logical_and(c < n_new, c <= t)


SROWS = N_HEADS * 8


def _moba_sample_kernel(pt_ref, q_ref, kn_ref, vn_ref, bt_ref, bfar_ref, bn_ref, e_ref, k_hbm, v_hbm, o_ref,
                        kbuf, vbuf, sem, km_ref, *, n_pages, chunk_pages, n_new):
    b = pl.program_id(0)
    nseq = pl.num_programs(0)
    slot = b % 2

    def fetch(seq, sl, start):
        _pages_copy(pt_ref, seq, 0, n_pages, k_hbm, kbuf, sl, sem.at[0, sl], start)
        _pages_copy(pt_ref, seq, 0, n_pages, v_hbm, vbuf, sl, sem.at[1, sl], start)

    @pl.when(b == 0)
    def _first():
        fetch(0, 0, True)

    @pl.when(b + 1 < nseq)
    def _next():
        fetch(b + 1, 1 - slot, True)

    fetch(b, slot, False)

    qs = q_ref[0]
    ck = chunk_pages * PAGE
    nbc = ck // MOBA_BLOCK
    nchunk = n_pages // chunk_pages

    def pages(c):
        return pl.ds(c * chunk_pages, chunk_pages)

    def mean_body(c, carry):
        kc = kbuf[slot, pages(c)].reshape(nbc, MOBA_BLOCK, LANE)
        km_ref[pl.ds(pl.multiple_of(c * nbc, nbc), nbc), :] = jnp.sum(kc, axis=1) * (1.0 / MOBA_BLOCK)
        return carry

    lax.fori_loop(0, nchunk, mean_body, 0)
    gate = lax.dot_general(qs, km_ref[...].astype(BF16), _NT, preferred_element_type=F32)
    sel = _top_blocks(gate, gate.shape[1], MOBA_TOPK).astype(BF16)

    def body(c, carry):
        kc = kbuf[slot, pages(c)].reshape(ck, LANE).astype(BF16)
        vc = vbuf[slot, pages(c)].reshape(ck, LANE).astype(BF16)
        s = lax.dot_general(qs, kc, _NT, preferred_element_type=F32)
        bias = jnp.where(c == nchunk - 1, bt_ref[...], bfar_ref[:, 0:1])
        keep = jnp.dot(sel, e_ref[:, pl.ds(pl.multiple_of(c * ck, ck), ck)], preferred_element_type=F32) > 0.5
        return _online_update(jnp.where(keep, s + bias, NEG), vc, *carry)

    init = (jnp.full((SROWS, 1), NEG, F32), jnp.zeros((SROWS, 1), F32), jnp.zeros((SROWS, LANE), F32))
    m, l, acc = lax.fori_loop(0, nchunk, body, init)
    s = lax.dot_general(qs, kn_ref[0].astype(BF16), _NT, preferred_element_type=F32) + bn_ref[:, 0:8]
    s = jnp.where(_new_key_mask(SROWS, 8, n_new), s, NEG)
    m, l, acc = _online_update(s, vn_ref[0].astype(BF16), m, l, acc)
    o_ref[0] = (acc / l).astype(o_ref.dtype)


def _moba_sample(page_table, qs, knew, vnew, bias_tail, bias_far, bias_new, expand, cache_k, cache_v, n_new):
    ndb, n_pages = page_table.shape
    chunk_pages = bias_tail.shape[1] // PAGE
    nb = n_pages * PAGE // MOBA_BLOCK
    seq = lambda b, pt: (b, 0, 0)
    const = lambda b, pt: (0, 0)
    grid_spec = pltpu.PrefetchScalarGridSpec(
        num_scalar_prefetch=1,
        grid=(ndb,),
        in_specs=[pl.BlockSpec((1, SROWS, LANE), seq), pl.BlockSpec((1, 8, LANE), seq), pl.BlockSpec((1, 8, LANE), seq),
                  pl.BlockSpec(bias_tail.shape, const), pl.BlockSpec(bias_far.shape, const),
                  pl.BlockSpec(bias_new.shape, const), pl.BlockSpec(expand.shape, const),
                  pl.BlockSpec(memory_space=pl.ANY), pl.BlockSpec(memory_space=pl.ANY)],
        out_specs=pl.BlockSpec((1, SROWS, LANE), seq),
        scratch_shapes=[pltpu.VMEM((2, n_pages, PAGE, LANE), F32), pltpu.VMEM((2, n_pages, PAGE, LANE), F32),
                        pltpu.SemaphoreType.DMA((2, 2)), pltpu.VMEM((nb, LANE), F32)])
    return pl.pallas_call(
        functools.partial(_moba_sample_kernel, n_pages=n_pages, chunk_pages=chunk_pages, n_new=n_new),
        out_shape=jax.ShapeDtypeStruct((ndb, SROWS, LANE), BF16),
        grid_spec=grid_spec,
        compiler_params=_cparams(("arbitrary",)),
        name="moba_sample",
    )(page_table, qs, knew, vnew, bias_tail, bias_far, bias_new, expand, cache_k, cache_v)


def _mla_absorb_kernel(q_ref, gk_ref, wuk_ref, o_ref):
    for h in range(N_HEADS):
        blk = slice(h * LANE, (h + 1) * LANE)
        qg = (q_ref[:, blk].astype(F32) * gk_ref[...]).astype(BF16)
        o_ref[:, h * B_KV_RANK:(h + 1) * B_KV_RANK] = lax.dot_general(
            qg, wuk_ref[:, blk], _NT, preferred_element_type=F32).astype(o_ref.dtype)


def _mla_absorb(qb, gk_nope, wuk):
    n = qb.shape[0]
    return pl.pallas_call(
        _mla_absorb_kernel,
        out_shape=jax.ShapeDtypeStruct((n, N_HEADS * B_KV_RANK), BF16),
        name="mla_absorb_q",
    )(qb, gk_nope, wuk)


def _mla_sample_kernel(pt_ref, qa_ref, qp_ref, qps_ref, cn_ref, pn_ref, wuk_ref, rsel_ref, gkp_ref, cb_ref, sb_ref,
                       co_ref, so_ref, c_hbm, p_hbm, o_ref, cbuf, pbuf, sem, m_ref, l_ref, acc_ref,
                       *, chunk_pages, sub_pages, n_new):
    c = pl.program_id(1)
    nch = pl.num_programs(1)
    slot = _stream_chunks(pt_ref, chunk_pages, [(c_hbm, cbuf, sem.at[0]), (p_hbm, pbuf, sem.at[1])])

    @pl.when(c == 0)
    def _init():
        m_ref[...] = jnp.full(m_ref.shape, NEG, F32)
        l_ref[...] = jnp.zeros(l_ref.shape, F32)
        acc_ref[...] = jnp.zeros(acc_ref.shape, F32)

    qabs = qa_ref[0]
    qp, qps = qp_ref[0], qps_ref[0]
    ones = jnp.ones((SROWS, B_ROPE), BF16)

    def rotated_queries(base_row):
        cb, sb = cb_ref[base_row], sb_ref[base_row]
        return (qp * cb + qps * sb).astype(BF16), (qps * cb - qp * sb).astype(BF16)

    def scores(cf, kp, off0, n, qc, qcs):
        c16 = cf.astype(BF16)
        kn = jnp.dot(c16, wuk_ref[...], preferred_element_type=F32)
        ssq = lax.dot_general(rsel_ref[...], (kn * kn).astype(BF16), _NT, preferred_element_type=F32)
        ssq = ssq + lax.dot_general(ones, (kp * kp).astype(BF16), _NT, preferred_element_type=F32)
        r = lax.rsqrt(ssq * (1.0 / B_QK) + EPS)
        xg = kp * gkp_ref[...]
        xc = (xg * co_ref[pl.ds(off0, n), :]).astype(BF16)
        xs = (xg * so_ref[pl.ds(off0, n), :]).astype(BF16)
        s = (lax.dot_general(qabs, c16, _NT, preferred_element_type=F32)
             + lax.dot_general(qc, xc, _NT, preferred_element_type=F32)
             + lax.dot_general(qcs, xs, _NT, preferred_element_type=F32))
        return s * r, c16

    qc, qcs = rotated_queries(c)
    sub = sub_pages * PAGE
    for u in range(chunk_pages // sub_pages):
        pg = pl.ds(u * sub_pages, sub_pages)
        cf = cbuf[slot, pg].reshape(sub, B_KV_RANK)
        kp = pbuf[slot, pg].reshape(sub, B_ROPE)
        s, c16 = scores(cf, kp, u * sub, sub, qc, qcs)
        _softmax_step(s, c16, m_ref, l_ref, acc_ref)

    @pl.when(c == nch - 1)
    def _new():
        qn, qns = rotated_queries(nch)
        s, c16 = scores(cn_ref[0], pn_ref[0], 0, 8, qn, qns)
        s = jnp.where(_new_key_mask(SROWS, 8, n_new), s, NEG)
        _softmax_step(s, c16, m_ref, l_ref, acc_ref)
        o_ref[0] = acc_ref[...] / l_ref[...]


def _mla_sample(page_table, qabs, qpe, qpes, ckv_new, kpe_new, wuk, rsel, gkpe, tables, cache_ckv, cache_kpe, n_new):
    ndb, n_pages = page_table.shape
    cb, sb, co, so = tables
    chunk_pages = co.shape[0] // PAGE
    sub_pages = min(4, chunk_pages)
    nch = n_pages // chunk_pages
    seq = lambda b, c, pt: (b, 0, 0)
    const2 = lambda b, c, pt: (0, 0)
    const3 = lambda b, c, pt: (0, 0, 0)
    grid_spec = pltpu.PrefetchScalarGridSpec(
        num_scalar_prefetch=1,
        grid=(ndb, nch),
        in_specs=[pl.BlockSpec((1, SROWS, B_KV_RANK), seq), pl.BlockSpec((1, SROWS, B_ROPE), seq),
                  pl.BlockSpec((1, SROWS, B_ROPE), seq), pl.BlockSpec((1, 8, B_KV_RANK), seq),
                  pl.BlockSpec((1, 8, B_ROPE), seq), pl.BlockSpec(wuk.shape, const2), pl.BlockSpec(rsel.shape, const2),
                  pl.BlockSpec(gkpe.shape, const2), pl.BlockSpec(cb.shape, const3), pl.BlockSpec(sb.shape, const3),
                  pl.BlockSpec(co.shape, const2), pl.BlockSpec(so.shape, const2),
                  pl.BlockSpec(memory_space=pl.ANY), pl.BlockSpec(memory_space=pl.ANY)],
        out_specs=pl.BlockSpec((1, SROWS, B_KV_RANK), seq),
        scratch_shapes=[pltpu.VMEM((2, chunk_pages, PAGE, B_KV_RANK), F32), pltpu.VMEM((2, chunk_pages, PAGE, B_ROPE), F32),
                        pltpu.SemaphoreType.DMA((2, 2)), pltpu.VMEM((SROWS, 1), F32), pltpu.VMEM((SROWS, 1), F32),
                        pltpu.VMEM((SROWS, B_KV_RANK), F32)])
    return pl.pallas_call(
        functools.partial(_mla_sample_kernel, chunk_pages=chunk_pages, sub_pages=sub_pages, n_new=n_new),
        out_shape=jax.ShapeDtypeStruct((ndb, SROWS, B_KV_RANK), F32),
        grid_spec=grid_spec,
        compiler_params=_cparams(("arbitrary", "arbitrary")),
        name="mla_sample",
    )(page_table, qabs, qpe, qpes, ckv_new, kpe_new, wuk, rsel, gkpe, cb, sb, co, so, cache_ckv, cache_kpe)


def _mla_value_kernel(o_ref, wuv_ref, out_ref):
    for h in range(N_HEADS):
        out_ref[:, h * LANE:(h + 1) * LANE] = jnp.dot(
            o_ref[:, h * B_KV_RANK:(h + 1) * B_KV_RANK].astype(BF16), wuv_ref[:, h * LANE:(h + 1) * LANE],
            preferred_element_type=F32).astype(out_ref.dtype)


def _mla_value(olat, wuv):
    n = olat.shape[0]
    return pl.pallas_call(
        _mla_value_kernel,
        out_shape=jax.ShapeDtypeStruct((n, 1024), BF16),
        name="mla_value_up",
    )(olat, wuv)


def _mla_rope_tables(n_pages, chunk_pages):
    half = B_ROPE // 2
    inv = ROPE_THETA ** (-np.arange(half, dtype=np.float64) / half)
    ck = chunk_pages * PAGE
    nch = n_pages // chunk_pages
    base = (np.arange(nch + 1) * ck)[:, None] * inv[None, :]
    off = np.arange(ck)[:, None] * inv[None, :]
    cb = np.concatenate([np.cos(base), np.cos(base)], axis=1)[:, None, :]
    sb = np.concatenate([np.sin(base), -np.sin(base)], axis=1)[:, None, :]
    co = np.concatenate([np.cos(off), np.cos(off)], axis=1)
    so = np.concatenate([np.sin(off), -np.sin(off)], axis=1)
    return tuple(jnp.asarray(a, F32) for a in (cb, sb, co, so))


def _dsa_index_kernel(pt_ref, qi_ref, wi_ref, kin_ref, k_hbm, s_ref, sn_ref, kbuf, sem, *, chunk_pages, n_new):
    c = pl.program_id(1)
    slot = _stream_chunks(pt_ref, chunk_pages, [(k_hbm, kbuf, sem)])
    qi = qi_ref[0]
    wi = wi_ref[0]

    def index(keys):
        sc = lax.dot_general(qi, keys.astype(BF16), _NT, preferred_element_type=F32)
        sc = jnp.maximum(sc, 0.0) * wi
        return jnp.sum(sc.reshape(N_HEADS, 8, sc.shape[1]), axis=0)

    ck = chunk_pages * PAGE
    s_ref[0] = index(kbuf[slot].reshape(ck, IDX_DIM))

    @pl.when(c == pl.num_programs(1) - 1)
    def _new():
        sn_ref[0] = jnp.where(_new_key_mask(8, LANE, n_new), index(kin_ref[0]), NEG)


def _dsa_index(page_table, qis, wis, ki_new, cache_kidx, chunk_pages, n_new):
    ndb, n_pages = page_table.shape
    nch = n_pages // chunk_pages
    ck = chunk_pages * PAGE
    seq = lambda b, c, pt: (b, 0, 0)
    grid_spec = pltpu.PrefetchScalarGridSpec(
        num_scalar_prefetch=1,
        grid=(ndb, nch),
        in_specs=[pl.BlockSpec((1, SROWS, IDX_DIM), seq), pl.BlockSpec((1, SROWS, 1), seq),
                  pl.BlockSpec((1, LANE, IDX_DIM), seq), pl.BlockSpec(memory_space=pl.ANY)],
        out_specs=[pl.BlockSpec((1, 8, ck), lambda b, c, pt: (b, 0, c)), pl.BlockSpec((1, 8, LANE), seq)],
        scratch_shapes=[pltpu.VMEM((2, chunk_pages, PAGE, IDX_DIM), F32), pltpu.SemaphoreType.DMA((2,))])
    return pl.pallas_call(
        functools.partial(_dsa_index_kernel, chunk_pages=chunk_pages, n_new=n_new),
        out_shape=[jax.ShapeDtypeStruct((ndb, 8, n_pages * PAGE), F32), jax.ShapeDtypeStruct((ndb, 8, LANE), F32)],
        grid_spec=grid_spec,
        compiler_params=_cparams(("arbitrary", "arbitrary")),
        name="dsa_sample_index",
    )(page_table, qis, wis, ki_new, cache_kidx)


def _dsa_select_kernel(s_ref, sn_ref, m_ref, mn_ref, *, topk):
    x, xn = s_ref[0], sn_ref[0]
    big = -NEG

    def rmax(a):
        return jnp.max(a, axis=-1, keepdims=True)

    def rmin(a):
        return jnp.min(jnp.where(a > 0.5 * NEG, a, big), axis=-1, keepdims=True)

    mx = jnp.maximum(rmax(x), rmax(xn))
    mn = jnp.minimum(rmin(x), rmin(xn))

    def bisect(_, carry):
        lo, hi = carry
        mid = lo + (hi - lo) * 0.5
        cnt = (jnp.sum(jnp.where(x >= mid, 1.0, 0.0), axis=-1, keepdims=True)
               + jnp.sum(jnp.where(xn >= mid, 1.0, 0.0), axis=-1, keepdims=True))
        ge = cnt >= float(topk)
        return jnp.where(ge, mid, lo), jnp.where(ge, hi, mid)

    thr, _ = lax.fori_loop(0, BISECT_ITERS, bisect, (mn, mx + (mx - mn) + 1.0))
    m_ref[0] = jnp.where(x >= thr, 0.0, NEG)
    mn_ref[0] = jnp.where(xn >= thr, 0.0, NEG)


def _dsa_select(scores, scores_new, topk):
    ndb, _, lc = scores.shape
    seq = lambda b: (b, 0, 0)
    return pl.pallas_call(
        functools.partial(_dsa_select_kernel, topk=topk),
        out_shape=[jax.ShapeDtypeStruct(scores.shape, F32), jax.ShapeDtypeStruct(scores_new.shape, F32)],
        grid=(ndb,),
        in_specs=[pl.BlockSpec((1, 8, lc), seq), pl.BlockSpec((1, 8, LANE), seq)],
        out_specs=[pl.BlockSpec((1, 8, lc), seq), pl.BlockSpec((1, 8, LANE), seq)],
        compiler_params=_cparams(("parallel",)),
        name="dsa_sample_select",
    )(scores, scores_new)


def _dsa_sample_kernel(pt_ref, q_ref, mk_ref, mkn_ref, kn_ref, vn_ref, bt_ref, bfar_ref, bn_ref, k_hbm, v_hbm, o_ref,
                       kbuf, vbuf, sem, m_ref, l_ref, acc_ref, *, chunk_pages):
    c = pl.program_id(1)
    nch = pl.num_programs(1)
    slot = _stream_chunks(pt_ref, chunk_pages, [(k_hbm, kbuf, sem.at[0]), (v_hbm, vbuf, sem.at[1])])

    @pl.when(c == 0)
    def _init():
        m_ref[...] = jnp.full(m_ref.shape, NEG, F32)
        l_ref[...] = jnp.zeros(l_ref.shape, F32)
        acc_ref[...] = jnp.zeros(acc_ref.shape, F32)

    half = SROWS // N_KV_HEADS
    ck = chunk_pages * PAGE

    def attend(k, v, bias, mask):
        n = k.shape[0]
        for kv in range(N_KV_HEADS):
            rows = slice(kv * half, (kv + 1) * half)
            lanes = slice(kv * LANE, (kv + 1) * LANE)
            s = lax.dot_general(q_ref[0, rows, :], k[:, lanes].astype(BF16), _NT, preferred_element_type=F32)
            s = (s + bias[rows, :]).reshape(GROUP, 8, n) + mask[None]
            _softmax_step(s.reshape(half, n), v[:, lanes].astype(BF16), m_ref.at[rows, :], l_ref.at[rows, :],
                          acc_ref.at[rows, :])

    bias = jnp.where(c == nch - 1, bt_ref[...], bfar_ref[:, 0:1])
    attend(kbuf[slot].reshape(ck, 2 * LANE), vbuf[slot].reshape(ck, 2 * LANE), bias, mk_ref[0])

    @pl.when(c == nch - 1)
    def _new():
        attend(kn_ref[0], vn_ref[0], bn_ref[:, 0:8], mkn_ref[0][:, 0:8])
        o_ref[0] = (acc_ref[...] / l_ref[...]).astype(o_ref.dtype)


def _dsa_sample(page_table, qs, mask, mask_new, knew, vnew, bias_tail, bias_far, bias_new, cache_k, cache_v):
    ndb, n_pages = page_table.shape
    ck = bias_tail.shape[1]
    chunk_pages = ck // PAGE
    nch = n_pages // chunk_pages
    seq = lambda b, c, pt: (b, 0, 0)
    const = lambda b, c, pt: (0, 0)
    grid_spec = pltpu.PrefetchScalarGridSpec(
        num_scalar_prefetch=1,
        grid=(ndb, nch),
        in_specs=[pl.BlockSpec((1, SROWS, LANE), seq), pl.BlockSpec((1, 8, ck), lambda b, c, pt: (b, 0, c)),
                  pl.BlockSpec((1, 8, LANE), seq), pl.BlockSpec((1, 8, 2 * LANE), seq), pl.BlockSpec((1, 8, 2 * LANE), seq),
                  pl.BlockSpec(bias_tail.shape, const), pl.BlockSpec(bias_far.shape, const),
                  pl.BlockSpec(bias_new.shape, const),
                  pl.BlockSpec(memory_space=pl.ANY), pl.BlockSpec(memory_space=pl.ANY)],
        out_specs=pl.BlockSpec((1, SROWS, LANE), seq),
        scratch_shapes=[pltpu.VMEM((2, chunk_pages, PAGE, 2 * LANE), F32), pltpu.VMEM((2, chunk_pages, PAGE, 2 * LANE), F32),
                        pltpu.SemaphoreType.DMA((2, 2)), pltpu.VMEM((SROWS, 1), F32), pltpu.VMEM((SROWS, 1), F32),
                        pltpu.VMEM((SROWS, LANE), F32)])
    return pl.pallas_call(
        functools.partial(_dsa_sample_kernel, chunk_pages=chunk_pages),
        out_shape=jax.ShapeDtypeStruct((ndb, SROWS, LANE), BF16),
        grid_spec=grid_spec,
        compiler_params=_cparams(("arbitrary", "arbitrary")),
        name="dsa_sample",
    )(page_table, qs, mask, mask_new, knew, vnew, bias_tail, bias_far, bias_new, cache_k, cache_v)


def _stack_heads(a, ndb, t, width):
    a = a.reshape(ndb, t, N_HEADS, width).transpose(0, 2, 1, 3)
    a = jnp.pad(a, ((0, 0), (0, 0), (0, 8 - t), (0, 0)))
    return a.reshape(ndb, SROWS, width)


def _unstack_heads(a, ndb, t):
    width = a.shape[-1]
    a = a.reshape(ndb, N_HEADS, 8, width)[:, :, :t]
    return a.transpose(0, 2, 1, 3).reshape(ndb * t, N_HEADS * width)


def _pad_rows(a, ndb, t, rows):
    a = a.reshape(ndb, t, a.shape[-1])
    return jnp.pad(a, ((0, 0), (0, rows - t), (0, 0)))


def _sample_trunk(x, mods, n_prompt, rel_bias, norm_g, even_w, out_even_w, odd_w, out_odd_w, page_table, caches,
                  gb_k, w_uk):
    ndb, t, _ = x.shape
    n_pages = page_table.shape[1]
    past = n_pages * PAGE
    n = ndb * t
    tm = _token_tile(n)
    x2 = x.reshape(n, D_MODEL)
    cache_mk, cache_mv, cache_ckv, cache_kpe, cache_dk, cache_dv, cache_di = caches
    pos = past + (np.arange(n) % t)
    rope = tuple(tab.reshape(n // tm, tm, LANE) for tab in _rope_tables(pos))

    def row_mod(m):
        m = jnp.repeat(m[n_prompt:n_prompt + ndb], t, axis=0)
        return m.reshape(n // tm, tm, 3 * D_MODEL)

    mod0, mod1 = row_mod(mods[0]), row_mod(mods[1])
    chunk_pages = min(16, n_pages)
    ck = chunk_pages * PAGE
    bias_tail = _bias_tile(rel_bias, 8, ck, ck)
    bias_far = _bias_tile(rel_bias, 8, LANE, past + FAR_DIST)
    bias_new = _bias_tile(rel_bias, 8, LANE, 0)

    qa, ka, va, sga, qb, ckv, kpe, kb, vb, sgb = _even_proj(x2, mod0, norm_g[0:1], even_w, rope, 1)
    nb = past // MOBA_BLOCK
    expand = jnp.asarray(np.arange(past)[None, :] // MOBA_BLOCK == np.arange(nb)[:, None], BF16)
    oa = _moba_sample(page_table, _stack_heads(qa, ndb, t, LANE), _pad_rows(ka, ndb, t, 8), _pad_rows(va, ndb, t, 8),
                      bias_tail, bias_far, bias_new, expand, cache_mk, cache_mv, t)
    oa = _unstack_heads(oa, ndb, t)
    wuk_full = w_uk.reshape(B_KV_RANK, N_HEADS * B_NOPE).astype(BF16)
    gk_nope = jnp.zeros((1, LANE), F32).at[0, 0:B_NOPE].set(gb_k[0:B_NOPE])
    qabs = _mla_absorb(qb, gk_nope, even_w[1])
    qabs = _stack_heads(qabs, ndb, t, B_KV_RANK)
    qpe = qb.reshape(n, N_HEADS, LANE)[:, :, B_NOPE:B_QK].astype(F32)
    half = B_ROPE // 2
    qpes = jnp.concatenate([qpe[..., half:], qpe[..., :half]], axis=-1)
    qpe = _stack_heads(qpe.reshape(n, -1), ndb, t, B_ROPE)
    qpes = _stack_heads(qpes.reshape(n, -1), ndb, t, B_ROPE)
    rsel = jnp.asarray(np.arange(N_HEADS * B_NOPE)[None, :] // B_NOPE == (np.arange(SROWS) // 8)[:, None], BF16)
    olat = _mla_sample(page_table, qabs, qpe, qpes, _pad_rows(ckv, ndb, t, 8), _pad_rows(kpe, ndb, t, 8), wuk_full, rsel,
                       gb_k[None, B_NOPE:B_QK], _mla_rope_tables(n_pages, chunk_pages), cache_ckv, cache_kpe, t)
    ob = _mla_value(_unstack_heads(olat, ndb, t), even_w[2])
    wa, wb = out_even_w
    x2 = _out_proj(x2, mod0, [(oa, sga, wa), (ob, sgb, wb)], tm, 1)

    qc, kc, vc, sgc, qi, ki, kib, wi, kcb, vcb = _odd_proj(x2, mod1, norm_g[1:2], odd_w, tm, 1)
    qis = _stack_heads(qi.reshape(n, N_HEADS, LANE)[:, :, 0:IDX_DIM].reshape(n, -1), ndb, t, IDX_DIM)
    wis = _stack_heads(wi[:, 0:IDX_HEADS], ndb, t, 1)
    scores, scores_new = _dsa_index(page_table, qis, wis, _pad_rows(ki, ndb, t, LANE), cache_di, chunk_pages, t)
    mask, mask_new = _dsa_select(scores, scores_new, min(DSA_TOPK, (past + t) // 4))
    oc = _dsa_sample(page_table, _stack_heads(qc, ndb, t, LANE), mask, mask_new, _pad_rows(kc, ndb, t, 8),
                     _pad_rows(vc, ndb, t, 8), bias_tail, bias_far, bias_new, cache_dk, cache_dv)
    x2 = _out_proj(x2, mod1, [(_unstack_heads(oc, ndb, t), sgc, out_odd_w)], tm, 1)

    rows = (ka.reshape(1, ndb, t, N_KV_HEADS, A_HEAD_DIM), va.reshape(1, ndb, t, N_KV_HEADS, A_HEAD_DIM),
            ckv.reshape(1, ndb, t, B_KV_RANK), kpe.reshape(1, ndb, t, B_ROPE),
            kc.reshape(1, ndb, t, N_KV_HEADS, C_HEAD_DIM), vc.reshape(1, ndb, t, N_KV_HEADS, C_HEAD_DIM),
            ki.reshape(1, ndb, t, IDX_DIM))
    return x2.reshape(ndb, t, D_MODEL), rows


def kernel(x_prompt, x_sample, cache_moba_k, cache_moba_v, cache_mla_ckv, cache_mla_kpe, cache_dsa_k, cache_dsa_v,
           cache_dsa_kidx, page_table, c_prompt, c_sample, rel_bias, norm_g, ada_w, ada_b, w_in_even, ga_q, ga_k,
           gb_q, gb_k, g_ckv, w_uk, w_uv, w_out_even, w_in_odd, gc_q, gc_k, w_out_odd):
    nb, ndb = x_prompt.shape[0], x_sample.shape[0]
    pad = (-(nb + ndb)) % 8
    c_all = jnp.concatenate([c_prompt, c_sample, jnp.zeros((pad, D_MODEL), F32)], axis=0)
    mods = _modulation(c_all, ada_w, ada_b)
    tq = MOBA_BLOCK
    rel_tiles = (_bias_tile(rel_bias, tq, tq, 0), _bias_tile(rel_bias, tq, tq, tq),
                 _bias_tile(rel_bias, tq, tq, tq + FAR_DIST))
    even_w = _even_weights(w_in_even[0], ga_q[0], ga_k[0], gb_q[0], gb_k[0], g_ckv[0], w_uk[0], w_uv[0])
    out_even_w = _out_weights_even(w_out_even[0])
    odd_w = _odd_weights(w_in_odd[0], gc_q[0], gc_k[0])
    out_odd_w = w_out_odd[0].astype(BF16)
    y_p, rows_p = _prompt_trunk(x_prompt, mods, rel_tiles, norm_g, even_w, out_even_w, odd_w, out_odd_w)
    pool = cache_moba_k.shape[0] * cache_moba_k.shape[1]
    caches = (cache_moba_k.reshape(pool, PAGE, LANE), cache_moba_v.reshape(pool, PAGE, LANE),
              cache_mla_ckv.reshape(pool, PAGE, B_KV_RANK), cache_mla_kpe.reshape(pool, PAGE, B_ROPE),
              cache_dsa_k.reshape(pool, PAGE, 2 * LANE), cache_dsa_v.reshape(pool, PAGE, 2 * LANE),
              cache_dsa_kidx.reshape(pool, PAGE, IDX_DIM))
    y_s, rows_s = _sample_trunk(x_sample, mods, nb, rel_bias, norm_g, even_w, out_even_w, odd_w, out_odd_w,
                                page_table, caches, gb_k[0], w_uk[0])
    return (y_p, y_s) + rows_p + rows_s
```

```python
import functools
import math

import jax
import jax.numpy as jnp
import numpy as np
from jax import lax
from jax.experimental import pallas as pl
from jax.experimental.pallas import tpu as pltpu

F32 = jnp.float32
BF16 = jnp.bfloat16

D_MODEL = 1024
N_HEADS = 8
N_KV_HEADS = 2
GROUP = N_HEADS // N_KV_HEADS
A_HEAD_DIM = 64
B_NOPE = 64
B_ROPE = 32
B_QK = B_NOPE + B_ROPE
B_VDIM = 64
B_KV_RANK = 256
C_HEAD_DIM = 128
IDX_HEADS = 8
IDX_DIM = 64
MOBA_BLOCK = 256
MOBA_TOPK = 3
DSA_TOPK = 256
PAGE = 128
REL_BUCKETS = 32
REL_MAX_DIST = 128
ROPE_THETA = 10000.0
EPS = 1e-6
LANE = 128
NEG = -1e30
BISECT_ITERS = 48
VMEM_LIMIT = 56 * 1024 * 1024


def _bucket_thresholds():
    n = np.arange(0, 4 * REL_MAX_DIST)
    exact = REL_BUCKETS // 2
    nf = np.maximum(n, 1).astype(np.float64)
    big = exact + (np.log(nf / exact) / math.log(REL_MAX_DIST / exact) * (REL_BUCKETS - exact)).astype(np.int32)
    b = np.where(n < exact, n, np.minimum(big, REL_BUCKETS - 1))
    return [int(np.argmax(b >= k)) for k in range(1, REL_BUCKETS)]


BUCKET_THR = _bucket_thresholds()
FAR_DIST = BUCKET_THR[-1]


def _cparams(sem):
    return pltpu.CompilerParams(dimension_semantics=sem, vmem_limit_bytes=VMEM_LIMIT)


def _bias_kernel(rel_ref, o_ref, *, tq, ncols, offset):
    t = lax.broadcasted_iota(jnp.int32, (tq, ncols), 0)
    c = lax.broadcasted_iota(jnp.int32, (tq, ncols), 1)
    dist = jnp.maximum(t - c + offset, 0)
    masks = [dist >= th for th in BUCKET_THR]
    for h in range(N_HEADS):
        v = jnp.full((tq, ncols), rel_ref[0, h], F32)
        for k, mk in enumerate(masks):
            v = jnp.where(mk, rel_ref[k + 1, h], v)
        o_ref[h * tq:(h + 1) * tq, :] = v


def _bias_tile(rel_bias, tq, ncols, offset):
    return pl.pallas_call(
        functools.partial(_bias_kernel, tq=tq, ncols=ncols, offset=offset),
        out_shape=jax.ShapeDtypeStruct((N_HEADS * tq, ncols), F32),
        in_specs=[pl.BlockSpec(memory_space=pltpu.SMEM)],
        out_specs=pl.BlockSpec(memory_space=pltpu.VMEM),
        name="t5_bias_tile",
    )(rel_bias)


def _mod_kernel(c_ref, w_ref, b_ref, o_ref):
    c = c_ref[...]
    s = c * jax.nn.sigmoid(c)
    o_ref[0] = jnp.dot(s.astype(BF16), w_ref[0].astype(BF16), preferred_element_type=F32) + b_ref[0]


def _modulation(c_all, ada_w, ada_b):
    depth, _, n3 = ada_w.shape
    r = c_all.shape[0]
    tn = 1024
    return pl.pallas_call(
        _mod_kernel,
        out_shape=jax.ShapeDtypeStruct((depth, r, n3), F32),
        grid=(depth, n3 // tn),
        in_specs=[pl.BlockSpec((r, D_MODEL), lambda l, j: (0, 0)),
                  pl.BlockSpec((1, D_MODEL, tn), lambda l, j: (l, 0, j)),
                  pl.BlockSpec((1, 1, tn), lambda l, j: (l, 0, j))],
        out_specs=pl.BlockSpec((1, r, tn), lambda l, j: (l, 0, j)),
        compiler_params=_cparams(("parallel", "parallel")),
        name="adaln_modulation",
    )(c_all, ada_w, ada_b.reshape(depth, 1, n3))


def _modulated_norm(x, mod, norm_g):
    xn = x * lax.rsqrt(jnp.mean(x * x, axis=-1, keepdims=True) + EPS)
    shift = mod[:, 0:D_MODEL]
    scale = mod[:, D_MODEL:2 * D_MODEL]
    return (xn * norm_g) * (1.0 + scale) + shift


def _silu(x):
    return x * jax.nn.sigmoid(x)


def _block_rms(xh, inv_n):
    ssq = jnp.sum(xh * xh, axis=-1, keepdims=True)
    return xh * lax.rsqrt(ssq * inv_n + EPS)


def _rope_block(xh, c, s1, s2):
    return xh * c + pltpu.roll(xh, LANE - 16, 1) * s1 + pltpu.roll(xh, 16, 1) * s2


def _even_proj_kernel(x_ref, mod_ref, ng_ref, w_ref, wuk_ref, wuv_ref, gaq_ref, gak_ref, gbq_ref, gbk_ref,
                      gckv_ref, rc_ref, rs1_ref, rs2_ref,
                      qa_ref, ka_ref, va_ref, sga_ref, qb_ref, ckv_ref, kpe_ref, kb_ref, vb_ref, sgb_ref):
    h = _modulated_norm(x_ref[...], mod_ref[0], ng_ref[...])
    z = jnp.dot(h.astype(BF16), w_ref[...], preferred_element_type=F32)
    o_qa, o_kv, o_ga, o_qb, o_ckv, o_kpe, o_gb = 0, 1024, 1280, 2304, 3328, 3584, 3712
    rc, rs1, rs2 = rc_ref[0], rs1_ref[0], rs2_ref[0]
    for hd in range(N_HEADS):
        blk = slice(hd * LANE, (hd + 1) * LANE)
        xq = z[:, o_qa + hd * LANE:o_qa + (hd + 1) * LANE]
        qa_ref[:, blk] = (_block_rms(xq, 1.0 / A_HEAD_DIM) * gaq_ref[:, blk] * (A_HEAD_DIM ** -0.5)).astype(qa_ref.dtype)
    ka = z[:, o_kv:o_kv + LANE]
    lane = lax.broadcasted_iota(jnp.int32, ka.shape, 1)
    lo = lane < A_HEAD_DIM
    k2 = ka * ka
    s0 = jnp.sum(jnp.where(lo, k2, 0.0), axis=-1, keepdims=True)
    s1 = jnp.sum(jnp.where(lo, 0.0, k2), axis=-1, keepdims=True)
    rk = lax.rsqrt(jnp.where(lo, s0, s1) * (1.0 / A_HEAD_DIM) + EPS)
    ka_ref[...] = ka * rk * gak_ref[...]
    va_ref[...] = z[:, o_kv + LANE:o_kv + 2 * LANE]
    sga_ref[...] = _silu(z[:, o_ga:o_ga + 1024]).astype(sga_ref.dtype)
    ckv = z[:, o_ckv:o_ckv + B_KV_RANK]
    ckv_n = ckv * lax.rsqrt(jnp.mean(ckv * ckv, axis=-1, keepdims=True) + EPS) * gckv_ref[...]
    ckv_ref[...] = ckv_n
    kpe_blk = z[:, o_kpe:o_kpe + LANE]
    kpe_ref[...] = kpe_blk[:, B_NOPE:B_NOPE + B_ROPE]
    ckv_b = ckv_n.astype(BF16)
    kn = jnp.dot(ckv_b, wuk_ref[...], preferred_element_type=F32)
    vb_ref[...] = jnp.dot(ckv_b, wuv_ref[...], preferred_element_type=F32).astype(vb_ref.dtype)
    for hd in range(N_HEADS):
        blk = slice(hd * LANE, (hd + 1) * LANE)
        xq = z[:, o_qb + hd * LANE:o_qb + (hd + 1) * LANE]
        qn = _block_rms(xq, 1.0 / B_QK) * gbq_ref[...]
        qb_ref[:, blk] = (_rope_block(qn, rc, rs1, rs2) * (B_QK ** -0.5)).astype(qb_ref.dtype)
        xk = kn[:, blk] + kpe_blk
        kk = _block_rms(xk, 1.0 / B_QK) * gbk_ref[...]
        kb_ref[:, blk] = _rope_block(kk, rc, rs1, rs2).astype(kb_ref.dtype)
    sgb_ref[...] = _silu(z[:, o_gb:o_gb + 1024]).astype(sgb_ref.dtype)


def _pad_heads(w, n_heads, width, offset=0):
    k = w.shape[0]
    w = w.reshape(k, n_heads, width)
    out = jnp.zeros((k, n_heads, LANE), w.dtype)
    out = out.at[:, :, offset:offset + width].set(w)
    return out.reshape(k, n_heads * LANE)


def _pad_kv_grouped(w):
    k = w.shape[0]
    w = w.reshape(k, N_KV_HEADS, GROUP, A_HEAD_DIM)
    out = jnp.zeros((k, N_KV_HEADS, GROUP, LANE), w.dtype)
    out = out.at[:, 0, :, 0:A_HEAD_DIM].set(w[:, 0])
    out = out.at[:, 1, :, A_HEAD_DIM:].set(w[:, 1])
    return out.reshape(k, N_HEADS * LANE)


def _rope_tables(positions):
    half = B_ROPE // 2
    inv = ROPE_THETA ** (-np.arange(half, dtype=np.float64) / half)
    ang = np.asarray(positions, dtype=np.float64)[:, None] * inv[None, :]
    cos, sin = np.cos(ang), np.sin(ang)
    n = ang.shape[0]
    c = np.ones((n, LANE)); s1 = np.zeros((n, LANE)); s2 = np.zeros((n, LANE))
    c[:, B_NOPE:B_NOPE + half] = cos
    c[:, B_NOPE + half:B_NOPE + B_ROPE] = cos
    s1[:, B_NOPE:B_NOPE + half] = -sin
    s2[:, B_NOPE + half:B_NOPE + B_ROPE] = sin
    return (jnp.asarray(c, F32), jnp.asarray(s1, F32), jnp.asarray(s2, F32))


def _even_weights(w_in, ga_q, ga_k, gb_q, gb_k, g_ckv, w_uk, w_uv):
    k = D_MODEL
    o = np.cumsum((0, 512, 128, 128, 512, 512, 256, 256, 32, 512))
    seg = [w_in[:, o[i]:o[i + 1]] for i in range(9)]
    w_qa, w_ka, w_va, w_ga, w_qbn, w_qbp, w_ckv, w_kpe, w_gb = seg
    qb = jnp.zeros((k, N_HEADS, LANE), w_in.dtype)
    qb = qb.at[:, :, 0:B_NOPE].set(w_qbn.reshape(k, N_HEADS, B_NOPE))
    qb = qb.at[:, :, B_NOPE:B_QK].set(w_qbp.reshape(k, N_HEADS, B_ROPE))
    kpe = jnp.zeros((k, LANE), w_in.dtype).at[:, B_NOPE:B_QK].set(w_kpe)
    w = jnp.concatenate([_pad_kv_grouped(w_qa), w_ka, w_va, _pad_kv_grouped(w_ga), qb.reshape(k, -1), w_ckv, kpe,
                         _pad_heads(w_gb, N_HEADS, B_VDIM)], axis=1).astype(BF16)
    wuk = _pad_heads(w_uk.reshape(B_KV_RANK, -1), N_HEADS, B_NOPE).astype(BF16)
    wuv = _pad_heads(w_uv.reshape(B_KV_RANK, -1), N_HEADS, B_VDIM).astype(BF16)
    gaq = _pad_kv_grouped(jnp.tile(ga_q, N_HEADS)[None, :])
    gak = jnp.tile(ga_k, N_KV_HEADS)[None, :]
    gbq = jnp.zeros((1, LANE), F32).at[0, 0:B_QK].set(gb_q)
    gbk = jnp.zeros((1, LANE), F32).at[0, 0:B_QK].set(gb_k)
    return w, wuk, wuv, gaq, gak, gbq, gbk, g_ckv[None, :]


def _token_tile(n):
    return 256 if n % 256 == 0 else n


def _even_proj(x, mod, norm_g, wts, rope, tiles_per_group):
    w, wuk, wuv, gaq, gak, gbq, gbk, gckv = wts
    n = x.shape[0]
    tm = rope[0].shape[1]
    ntab = rope[0].shape[0]
    r = mod.shape[1]
    ne = w.shape[1]
    row = lambda i: (i, 0)
    const = lambda i: (0, 0)
    tab = lambda i: (i % ntab, 0, 0)
    outs = [((n, 1024), BF16), ((n, LANE), F32), ((n, LANE), F32), ((n, 1024), BF16), ((n, 1024), BF16),
            ((n, B_KV_RANK), F32), ((n, B_ROPE), F32), ((n, 1024), BF16), ((n, 1024), BF16), ((n, 1024), BF16)]
    return pl.pallas_call(
        _even_proj_kernel,
        out_shape=[jax.ShapeDtypeStruct(s, d) for s, d in outs],
        grid=(n // tm,),
        in_specs=[pl.BlockSpec((tm, D_MODEL), row),
                  pl.BlockSpec((1, r, 3 * D_MODEL), lambda i: (i // tiles_per_group, 0, 0)),
                  pl.BlockSpec((1, D_MODEL), const),
                  pl.BlockSpec((D_MODEL, ne), const),
                  pl.BlockSpec((B_KV_RANK, 1024), const),
                  pl.BlockSpec((B_KV_RANK, 1024), const),
                  pl.BlockSpec((1, 1024), const),
                  pl.BlockSpec((1, LANE), const),
                  pl.BlockSpec((1, LANE), const),
                  pl.BlockSpec((1, LANE), const),
                  pl.BlockSpec((1, B_KV_RANK), const),
                  pl.BlockSpec((1, tm, LANE), tab),
                  pl.BlockSpec((1, tm, LANE), tab),
                  pl.BlockSpec((1, tm, LANE), tab)],
        out_specs=[pl.BlockSpec((tm, s[1]), row) for s, _ in outs],
        compiler_params=_cparams(("parallel",)),
        name="even_proj",
    )(x, mod, norm_g, w, wuk, wuv, gaq, gak, gbq, gbk, gckv, *rope)


def _odd_proj_kernel(x_ref, mod_ref, ng_ref, w_ref, gcq_ref, gck_ref,
                     qc_ref, kc_ref, vc_ref, sgc_ref, qi_ref, ki_ref, kib_ref, wi_ref, kcb_ref, vcb_ref):
    h = _modulated_norm(x_ref[...], mod_ref[0], ng_ref[...])
    z = jnp.dot(h.astype(BF16), w_ref[...], preferred_element_type=F32)
    o_qc, o_kc, o_vc, o_gc, o_qi, o_ki, o_wi = 0, 1024, 1280, 1536, 2560, 3584, 3712
    for hd in range(N_HEADS):
        blk = slice(hd * LANE, (hd + 1) * LANE)
        xq = z[:, o_qc + hd * LANE:o_qc + (hd + 1) * LANE]
        qc_ref[:, blk] = (_block_rms(xq, 1.0 / C_HEAD_DIM) * gcq_ref[...] * (C_HEAD_DIM ** -0.5)).astype(qc_ref.dtype)
    for kv in range(N_KV_HEADS):
        blk = slice(kv * LANE, (kv + 1) * LANE)
        kn = _block_rms(z[:, o_kc + kv * LANE:o_kc + (kv + 1) * LANE], 1.0 / C_HEAD_DIM) * gck_ref[...]
        kc_ref[:, blk] = kn
        kcb_ref[:, blk] = kn.astype(kcb_ref.dtype)
    vc = z[:, o_vc:o_vc + 2 * LANE]
    vc_ref[...] = vc
    vcb_ref[...] = vc.astype(vcb_ref.dtype)
    sgc_ref[...] = _silu(z[:, o_gc:o_gc + 1024]).astype(sgc_ref.dtype)
    qi_ref[...] = (z[:, o_qi:o_qi + 1024] * (IDX_DIM ** -0.5)).astype(qi_ref.dtype)
    ki_blk = z[:, o_ki:o_ki + LANE]
    ki_ref[...] = ki_blk[:, 0:IDX_DIM]
    kib_ref[...] = ki_blk.astype(kib_ref.dtype)
    wi_ref[...] = z[:, o_wi:o_wi + LANE] * (IDX_HEADS ** -0.5)


def _odd_weights(w_in, gc_q, gc_k):
    k = D_MODEL
    o = np.cumsum((0, 1024, 256, 256, 1024, 512, 64, 8))
    w_qc, w_kc, w_vc, w_gc, w_qi, w_ki, w_wi = [w_in[:, o[i]:o[i + 1]] for i in range(7)]
    ki = jnp.zeros((k, LANE), w_in.dtype).at[:, 0:IDX_DIM].set(w_ki)
    wi = jnp.zeros((k, LANE), w_in.dtype).at[:, 0:IDX_HEADS].set(w_wi)
    w = jnp.concatenate([w_qc, w_kc, w_vc, w_gc, _pad_heads(w_qi, IDX_HEADS, IDX_DIM), ki, wi], axis=1).astype(BF16)
    return w, gc_q[None, :], gc_k[None, :]


def _odd_proj(x, mod, norm_g, wts, tm, tiles_per_group):
    w, gcq, gck = wts
    n = x.shape[0]
    r = mod.shape[1]
    no = w.shape[1]
    row = lambda i: (i, 0)
    const = lambda i: (0, 0)
    outs = [((n, 1024), BF16), ((n, 2 * LANE), F32), ((n, 2 * LANE), F32), ((n, 1024), BF16), ((n, 1024), BF16),
            ((n, IDX_DIM), F32), ((n, LANE), BF16), ((n, LANE), F32), ((n, 2 * LANE), BF16), ((n, 2 * LANE), BF16)]
    return pl.pallas_call(
        _odd_proj_kernel,
        out_shape=[jax.ShapeDtypeStruct(s, d) for s, d in outs],
        grid=(n // tm,),
        in_specs=[pl.BlockSpec((tm, D_MODEL), row),
                  pl.BlockSpec((1, r, 3 * D_MODEL), lambda i: (i // tiles_per_group, 0, 0)),
                  pl.BlockSpec((1, D_MODEL), const),
                  pl.BlockSpec((D_MODEL, no), const),
                  pl.BlockSpec((1, LANE), const),
                  pl.BlockSpec((1, LANE), const)],
        out_specs=[pl.BlockSpec((tm, s[1]), row) for s, _ in outs],
        compiler_params=_cparams(("parallel",)),
        name="odd_proj",
    )(x, mod, norm_g, w, gcq, gck)


def _out_proj_kernel(*refs, n_terms):
    x_ref, mod_ref = refs[0], refs[1]
    o_ref = refs[-1]
    acc = None
    for i in range(n_terms):
        a_ref, g_ref, w_ref = refs[2 + 3 * i:5 + 3 * i]
        m = (a_ref[...].astype(F32) * g_ref[...].astype(F32)).astype(BF16)
        d = jnp.dot(m, w_ref[...], preferred_element_type=F32)
        acc = d if acc is None else acc + d
    gate = mod_ref[0][:, 2 * D_MODEL:3 * D_MODEL]
    o_ref[...] = x_ref[...] + gate * acc


def _out_proj(x, mod, terms, tm, tiles_per_group):
    n = x.shape[0]
    r = mod.shape[1]
    row = lambda i: (i, 0)
    const = lambda i: (0, 0)
    in_specs = [pl.BlockSpec((tm, D_MODEL), row),
                pl.BlockSpec((1, r, 3 * D_MODEL), lambda i: (i // tiles_per_group, 0, 0))]
    args = [x, mod]
    for a, g, w in terms:
        in_specs += [pl.BlockSpec((tm, a.shape[1]), row), pl.BlockSpec((tm, g.shape[1]), row),
                     pl.BlockSpec(w.shape, const)]
        args += [a, g, w]
    return pl.pallas_call(
        functools.partial(_out_proj_kernel, n_terms=len(terms)),
        out_shape=jax.ShapeDtypeStruct((n, D_MODEL), F32),
        grid=(n // tm,),
        in_specs=in_specs,
        out_specs=pl.BlockSpec((tm, D_MODEL), row),
        compiler_params=_cparams(("parallel",)),
        name="out_proj",
    )(*args)


def _kmean_kernel(k_ref, o_ref):
    k = k_ref[0]
    nb = k.shape[0] // MOBA_BLOCK
    o_ref[0] = jnp.sum(k.reshape(nb, MOBA_BLOCK, LANE), axis=1) * (1.0 / MOBA_BLOCK)


def _block_means(k):
    b, l, _ = k.shape
    nb = l // MOBA_BLOCK
    return pl.pallas_call(
        _kmean_kernel,
        out_shape=jax.ShapeDtypeStruct((b, nb, LANE), F32),
        grid=(b,),
        in_specs=[pl.BlockSpec((1, l, LANE), lambda i: (i, 0, 0))],
        out_specs=pl.BlockSpec((1, nb, LANE), lambda i: (i, 0, 0)),
        compiler_params=_cparams(("parallel",)),
        name="moba_block_means",
    )(k)


_NT = (((1,), (1,)), ((), ()))


def _top_blocks(gate, n_valid, n_pick):
    nb = gate.shape[1]
    lane = lax.broadcasted_iota(jnp.int32, gate.shape, 1)
    g = jnp.where(lane < n_valid, gate, NEG)
    sel = jnp.zeros(gate.shape, F32)
    for r in range(n_pick):
        mx = jnp.max(g, axis=-1, keepdims=True)
        idx = jnp.min(jnp.where(g == mx, lane, nb), axis=-1, keepdims=True)
        hit = lane == idx
        sel = jnp.where(jnp.logical_and(hit, n_valid > r), 1.0, sel)
        g = jnp.where(hit, NEG, g)
    return sel


def _lane_blocks(x):
    return [x[:, k * LANE:(k + 1) * LANE] for k in range(x.shape[1] // LANE)]


def _softmax_update(s, m_prev, l_prev):
    blocks = _lane_blocks(s)
    m_new = jnp.maximum(m_prev, jnp.max(functools.reduce(jnp.maximum, blocks), axis=-1, keepdims=True))
    alpha = jnp.exp(m_prev - m_new)
    ps = [jnp.exp(b - m_new) for b in blocks]
    l_new = alpha * l_prev + functools.reduce(jnp.add, ps)
    p = ps[0].astype(BF16) if len(ps) == 1 else jnp.concatenate([q.astype(BF16) for q in ps], axis=1)
    return m_new, l_new, alpha, p


def _rescale(acc, alpha):
    reps = acc.shape[1] // LANE
    return acc * (alpha if reps == 1 else jnp.concatenate([alpha] * reps, axis=1))


def _softmax_step(s, v, m_ref, l_ref, acc_ref, nt=False):
    m_new, l_new, alpha, p = _softmax_update(s, m_ref[...], l_ref[...])
    pv = (lax.dot_general(p, v, _NT, preferred_element_type=F32) if nt
          else jnp.dot(p, v, preferred_element_type=F32))
    acc_ref[...] = _rescale(acc_ref[...], alpha) + pv
    m_ref[...] = m_new
    l_ref[...] = l_new


def _normalise(acc, l):
    return acc / jnp.sum(l, axis=-1, keepdims=True)


def _moba_prompt_kernel(ii_ref, jj_ref, last_ref, q_ref, k_ref, v_ref, km_ref, bd_ref, bs_ref, bf_ref, o_ref,
                        qs_ref, sel_ref, m_ref, l_ref, acc_ref):
    step = pl.program_id(1)
    i = ii_ref[step]
    j = jj_ref[step]
    tq = MOBA_BLOCK
    k = k_ref[0].astype(BF16)
    v = v_ref[0].astype(BF16)

    @pl.when(j == i)
    def _own():
        for h in range(N_HEADS):
            qs_ref[h * tq:(h + 1) * tq, :] = q_ref[0, :, h * LANE:(h + 1) * LANE]
        gate = lax.dot_general(qs_ref[...], km_ref[0].astype(BF16), _NT, preferred_element_type=F32)
        sel_ref[...] = _top_blocks(gate, i, MOBA_TOPK).astype(sel_ref.dtype)
        s = lax.dot_general(qs_ref[...], k, _NT, preferred_element_type=F32) + bd_ref[...]
        s3 = s.reshape(N_HEADS, tq, tq)
        row = lax.broadcasted_iota(jnp.int32, (tq, tq), 0)
        col = lax.broadcasted_iota(jnp.int32, (tq, tq), 1)
        s = jnp.where((col <= row)[None], s3, NEG).reshape(N_HEADS * tq, tq)
        m, l, _, p = _softmax_update(s, jnp.full((N_HEADS * tq, LANE), NEG, F32), jnp.zeros((N_HEADS * tq, LANE), F32))
        m_ref[...] = m
        l_ref[...] = l
        acc_ref[...] = jnp.dot(p, v, preferred_element_type=F32)

    @pl.when(j != i)
    def _past():
        nb = sel_ref.shape[1]
        onehot = (lax.broadcasted_iota(jnp.int32, (nb, LANE), 0) == j).astype(BF16)
        chosen = jnp.dot(sel_ref[...], onehot, preferred_element_type=F32) > 0.5
        s = lax.dot_general(qs_ref[...], k, _NT, preferred_element_type=F32)
        bias = jnp.where(j == i - 1, bs_ref[...], bf_ref[...])
        s = jnp.where(jnp.concatenate([chosen] * (tq // LANE), axis=1), s + bias, NEG)
        _softmax_step(s, v, m_ref, l_ref, acc_ref)

    @pl.when(last_ref[step] == 1)
    def _fin():
        out = _normalise(acc_ref[...], l_ref[...])
        for h in range(N_HEADS):
            o_ref[0, :, h * LANE:(h + 1) * LANE] = out[h * tq:(h + 1) * tq, :].astype(o_ref.dtype)


def _causal_pairs(nq, own_first):
    ii, jj, last = [], [], []
    for i in range(nq):
        js = ([i] + list(range(i))) if own_first else list(range(i + 1))
        for n, j in enumerate(js):
            ii.append(i); jj.append(j); last.append(1 if n == len(js) - 1 else 0)
    return (jnp.asarray(ii, jnp.int32), jnp.asarray(jj, jnp.int32), jnp.asarray(last, jnp.int32))


def _moba_prompt(qa, ka, va, kmean, bias_diag, bias_sub, bias_far):
    b, t, _ = qa.shape
    tq = MOBA_BLOCK
    nq = t // tq
    nb = kmean.shape[1]
    ii, jj, last = _causal_pairs(nq, own_first=True)
    r = N_HEADS * tq
    grid_spec = pltpu.PrefetchScalarGridSpec(
        num_scalar_prefetch=3,
        grid=(b, int(ii.shape[0])),
        in_specs=[pl.BlockSpec((1, tq, 1024), lambda bb, s, ii, jj, la: (bb, ii[s], 0)),
                  pl.BlockSpec((1, tq, LANE), lambda bb, s, ii, jj, la: (bb, jj[s], 0)),
                  pl.BlockSpec((1, tq, LANE), lambda bb, s, ii, jj, la: (bb, jj[s], 0)),
                  pl.BlockSpec((1, nb, LANE), lambda bb, s, ii, jj, la: (bb, 0, 0)),
                  pl.BlockSpec((r, tq), lambda bb, s, ii, jj, la: (0, 0)),
                  pl.BlockSpec((r, tq), lambda bb, s, ii, jj, la: (0, 0)),
                  pl.BlockSpec((r, tq), lambda bb, s, ii, jj, la: (0, 0))],
        out_specs=pl.BlockSpec((1, tq, 1024), lambda bb, s, ii, jj, la: (bb, ii[s], 0)),
        scratch_shapes=[pltpu.VMEM((r, LANE), BF16), pltpu.VMEM((r, nb), BF16), pltpu.VMEM((r, LANE), F32),
                        pltpu.VMEM((r, LANE), F32), pltpu.VMEM((r, LANE), F32)])
    return pl.pallas_call(
        _moba_prompt_kernel,
        out_shape=jax.ShapeDtypeStruct((b, t, 1024), BF16),
        grid_spec=grid_spec,
        compiler_params=_cparams(("parallel", "arbitrary")),
        name="moba_prompt",
    )(ii, jj, last, qa, ka, va, kmean, bias_diag, bias_sub, bias_far)


def _mla_prompt_kernel(ii_ref, jj_ref, last_ref, q_ref, k_ref, v_ref, o_ref, m_ref, l_ref, acc_ref):
    step = pl.program_id(1)
    i = ii_ref[step]
    j = jj_ref[step]
    tq = q_ref.shape[1]

    @pl.when(j == 0)
    def _init():
        m_ref[...] = jnp.full(m_ref.shape, NEG, F32)
        l_ref[...] = jnp.zeros(l_ref.shape, F32)
        acc_ref[...] = jnp.zeros(acc_ref.shape, F32)

    row = lax.broadcasted_iota(jnp.int32, (tq, tq), 0)
    col = lax.broadcasted_iota(jnp.int32, (tq, tq), 1)
    visible = jnp.logical_or(col <= row, j < i)
    for h in range(N_HEADS):
        blk = slice(h * LANE, (h + 1) * LANE)
        s = lax.dot_general(q_ref[0, :, blk], k_ref[0, :, blk], _NT, preferred_element_type=F32)
        s = jnp.where(visible, s, NEG)
        _softmax_step(s, v_ref[0, :, blk], m_ref.at[h], l_ref.at[h], acc_ref.at[:, blk])

    @pl.when(last_ref[step] == 1)
    def _fin():
        for h in range(N_HEADS):
            blk = slice(h * LANE, (h + 1) * LANE)
            o_ref[0, :, blk] = _normalise(acc_ref[:, blk], l_ref[h]).astype(o_ref.dtype)


def _mla_prompt(qb, kb, vb):
    b, t, _ = qb.shape
    tq = 512 if t % 512 == 0 else 256
    nq = t // tq
    ii, jj, last = _causal_pairs(nq, own_first=False)
    grid_spec = pltpu.PrefetchScalarGridSpec(
        num_scalar_prefetch=3,
        grid=(b, int(ii.shape[0])),
        in_specs=[pl.BlockSpec((1, tq, 1024), lambda bb, s, ii, jj, la: (bb, ii[s], 0)),
                  pl.BlockSpec((1, tq, 1024), lambda bb, s, ii, jj, la: (bb, jj[s], 0)),
                  pl.BlockSpec((1, tq, 1024), lambda bb, s, ii, jj, la: (bb, jj[s], 0))],
        out_specs=pl.BlockSpec((1, tq, 1024), lambda bb, s, ii, jj, la: (bb, ii[s], 0)),
        scratch_shapes=[pltpu.VMEM((N_HEADS, tq, LANE), F32), pltpu.VMEM((N_HEADS, tq, LANE), F32),
                        pltpu.VMEM((tq, 1024), F32)])
    return pl.pallas_call(
        _mla_prompt_kernel,
        out_shape=jax.ShapeDtypeStruct((b, t, 1024), BF16),
        grid_spec=grid_spec,
        compiler_params=_cparams(("parallel", "arbitrary")),
        name="mla_prompt",
    )(ii, jj, last, qb, kb, vb)


def _row_stats(x):
    adm = x > 0.5 * NEG
    return (jnp.max(x, axis=-1, keepdims=True), jnp.min(jnp.where(adm, x, -NEG), axis=-1, keepdims=True),
            jnp.sum(jnp.where(adm, 1.0, 0.0), axis=-1, keepdims=True))


def _count_ge(x, t):
    return jnp.sum(jnp.where(x >= t, 1.0, 0.0), axis=-1, keepdims=True)


def _topk_bracket(stats, count_ge, topk):
    mx, mn, n_adm = stats
    kf = float(topk)

    def cond(state):
        it, _, _, c_lo, _ = state
        return jnp.logical_and(it < BISECT_ITERS, jnp.max(c_lo) > kf)

    def body(state):
        it, lo, hi, c_lo, c_hi = state
        rank = jnp.clip((c_lo - kf - 0.5) / (c_lo - c_hi), 0.02, 0.98)
        mid = lo + (hi - lo) * jnp.where(it % 2 == 0, 0.5, rank)
        c = count_ge(mid)
        ge = c >= kf
        return (it + 1, jnp.where(ge, mid, lo), jnp.where(ge, hi, mid), jnp.where(ge, c, c_lo), jnp.where(ge, c_hi, c))

    init = (jnp.int32(0), mn, mx + (mx - mn) + 1.0, n_adm, jnp.zeros_like(mn))
    _, lo, hi, c_lo, c_hi = lax.while_loop(cond, body, init)
    return lo, hi, c_lo, c_hi


def _drop_tied_excess(x, lo, hi, need, seen, upper):
    band = jnp.logical_and(x >= lo, x < hi)
    bf = jnp.where(band, 1.0, 0.0)
    prefix = jnp.dot(bf.astype(BF16), upper, preferred_element_type=F32) + seen
    x = jnp.where(jnp.logical_and(band, prefix >= need), NEG, x)
    return x, seen + jnp.sum(bf, axis=-1, keepdims=True)


def _strict_upper(n):
    return (lax.broadcasted_iota(jnp.int32, (n, n), 0) < lax.broadcasted_iota(jnp.int32, (n, n), 1)).astype(BF16)


def _dsa_prompt_kernel(q_ref, qi_ref, wi_ref, kib_ref, kc_ref, vc_ref, bd_ref, bs_ref, bf_ref, o_ref,
                       qs_ref, qis_ref, wis_ref, s_ref, thr_ref, m_ref, l_ref, acc_ref, *, topk, chunk):
    i = pl.program_id(1)
    tq = q_ref.shape[1]
    half = GROUP * tq
    nlb = tq // LANE
    wi = wi_ref[0]
    for h in range(N_HEADS):
        blk = slice(h * LANE, (h + 1) * LANE)
        qs_ref[h * tq:(h + 1) * tq, :] = q_ref[0, :, blk]
        qis_ref[h * tq:(h + 1) * tq, :] = qi_ref[0, :, blk]
        wis_ref[h * tq:(h + 1) * tq, :] = jnp.broadcast_to(wi[:, h:h + 1], (tq, LANE))
    s_ref[...] = jnp.full(s_ref.shape, NEG, F32)
    m_ref[...] = jnp.full(m_ref.shape, NEG, F32)
    l_ref[...] = jnp.zeros(l_ref.shape, F32)
    acc_ref[...] = jnp.zeros(acc_ref.shape, F32)

    def keys(j):
        return pl.ds(pl.multiple_of(j * tq, tq), tq)

    def index_tile(j):
        sc = lax.dot_general(qis_ref[...], kib_ref[0, keys(j), :], _NT, preferred_element_type=F32)
        sc = jnp.maximum(sc, 0.0) * jnp.concatenate([wis_ref[...]] * nlb, axis=1)
        return jnp.sum(sc.reshape(N_HEADS, tq, tq), axis=0)

    def index_body(j, c):
        s_ref[:, keys(j)] = index_tile(j)
        return c

    lax.fori_loop(0, i, index_body, 0)
    row = lax.broadcasted_iota(jnp.int32, (tq, tq), 0)
    col = lax.broadcasted_iota(jnp.int32, (tq, tq), 1)
    s_ref[:, keys(i)] = jnp.where(col <= row, index_tile(i), NEG)

    n_chunks = ((i + 1) * tq + chunk - 1) // chunk

    def chunk_of(c):
        return s_ref[:, pl.ds(pl.multiple_of(c * chunk, chunk), chunk)]

    def stat_body(c, carry):
        mx, mn, na = _row_stats(chunk_of(c))
        return jnp.maximum(carry[0], mx), jnp.minimum(carry[1], mn), carry[2] + na

    stats = lax.fori_loop(0, n_chunks, stat_body, (jnp.full((tq, 1), NEG, F32), jnp.full((tq, 1), -NEG, F32),
                                                    jnp.zeros((tq, 1), F32)))

    def count_ge(t):
        return lax.fori_loop(0, n_chunks, lambda c, a: a + _count_ge(chunk_of(c), t), jnp.zeros((tq, 1), F32))

    lo, hi, c_lo, c_hi = _topk_bracket(stats, count_ge, topk)

    @pl.when(jnp.max(c_lo) > float(topk))
    def _ties():
        upper = _strict_upper(tq)
        need = float(topk) - c_hi

        def fix(j, seen):
            x, seen = _drop_tied_excess(s_ref[:, keys(j)], lo, hi, need, seen, upper)
            s_ref[:, keys(j)] = x
            return seen

        lax.fori_loop(0, i + 1, fix, jnp.zeros((tq, 1), F32))

    thr_ref[...] = jnp.broadcast_to(lo, (tq, LANE))

    def attend(j, bias_ref):
        thr = thr_ref[...]
        keep = jnp.concatenate([b >= thr for b in _lane_blocks(s_ref[:, keys(j)])], axis=1)[None]
        kt = kc_ref[0, keys(j), :]
        vt = vc_ref[0, keys(j), :]
        for kv in range(N_KV_HEADS):
            rows = slice(kv * half, (kv + 1) * half)
            lanes = slice(kv * LANE, (kv + 1) * LANE)
            s = lax.dot_general(qs_ref[rows, :], kt[:, lanes], _NT, preferred_element_type=F32) + bias_ref[rows, :]
            s = jnp.where(keep, s.reshape(GROUP, tq, tq), NEG).reshape(half, tq)
            _softmax_step(s, vt[:, lanes], m_ref.at[rows, :], l_ref.at[rows, :], acc_ref.at[rows, :])

    def far_body(j, c):
        attend(j, bf_ref)
        return c

    lax.fori_loop(0, jnp.maximum(i - 1, 0), far_body, 0)

    @pl.when(i >= 1)
    def _sub():
        attend(i - 1, bs_ref)

    attend(i, bd_ref)
    out = _normalise(acc_ref[...], l_ref[...])
    for h in range(N_HEADS):
        o_ref[0, :, h * LANE:(h + 1) * LANE] = out[h * tq:(h + 1) * tq, :].astype(o_ref.dtype)


def _dsa_prompt(qc, qi, wi, kib, kcb, vcb, bias_diag, bias_sub, bias_far):
    b, t, _ = qc.shape
    tq = MOBA_BLOCK
    nq = t // tq
    r = N_HEADS * tq
    topk = min(DSA_TOPK, t // 4)
    chunk = min(1024, t)
    tile = lambda bb, i: (bb, i, 0)
    seq = lambda bb, i: (bb, 0, 0)
    const = lambda bb, i: (0, 0)
    return pl.pallas_call(
        functools.partial(_dsa_prompt_kernel, topk=topk, chunk=chunk),
        out_shape=jax.ShapeDtypeStruct((b, t, 1024), BF16),
        grid=(b, nq),
        in_specs=[pl.BlockSpec((1, tq, 1024), tile), pl.BlockSpec((1, tq, 1024), tile),
                  pl.BlockSpec((1, tq, LANE), tile),
                  pl.BlockSpec((1, t, LANE), seq), pl.BlockSpec((1, t, 2 * LANE), seq),
                  pl.BlockSpec((1, t, 2 * LANE), seq),
                  pl.BlockSpec((r, tq), const), pl.BlockSpec((r, tq), const), pl.BlockSpec((r, tq), const)],
        out_specs=pl.BlockSpec((1, tq, 1024), tile),
        scratch_shapes=[pltpu.VMEM((r, LANE), BF16), pltpu.VMEM((r, LANE), BF16), pltpu.VMEM((r, LANE), F32),
                        pltpu.VMEM((tq, t), F32), pltpu.VMEM((tq, LANE), F32), pltpu.VMEM((r, LANE), F32),
                        pltpu.VMEM((r, LANE), F32), pltpu.VMEM((r, LANE), F32)],
        compiler_params=_cparams(("parallel", "arbitrary")),
        name="dsa_prompt",
    )(qc, qi, wi, kib, kcb, vcb, bias_diag, bias_sub, bias_far)


def _out_weights_even(w_out):
    wa = _pad_kv_grouped(w_out[0:N_HEADS * A_HEAD_DIM].T).T.astype(BF16)
    wb = _pad_heads(w_out[N_HEADS * A_HEAD_DIM:].T, N_HEADS, B_VDIM).T.astype(BF16)
    return wa, wb


def _prompt_trunk(x, mods, rel_tiles, norm_g, even_w, out_even_w, odd_w, out_odd_w):
    b, t, _ = x.shape
    n = b * t
    tm = _token_tile(n)
    tpg = t // tm
    x2 = x.reshape(n, D_MODEL)
    bias_diag, bias_sub, bias_far = rel_tiles
    rope = tuple(tab.reshape(t // tm, tm, LANE) for tab in _rope_tables(np.arange(t)))
    mod0 = mods[0][:b].reshape(b, 1, 3 * D_MODEL)
    mod1 = mods[1][:b].reshape(b, 1, 3 * D_MODEL)

    qa, ka, va, sga, qb, ckv, kpe, kb, vb, sgb = _even_proj(x2, mod0, norm_g[0:1], even_w, rope, tpg)
    seq = lambda a: a.reshape(b, t, a.shape[-1])
    kmean = _block_means(seq(ka))
    oa = _moba_prompt(seq(qa), seq(ka), seq(va), kmean, bias_diag, bias_sub, bias_far)
    ob = _mla_prompt(seq(qb), seq(kb), seq(vb))
    wa, wb = out_even_w
    x2 = _out_proj(x2, mod0, [(oa.reshape(n, -1), sga, wa), (ob.reshape(n, -1), sgb, wb)], tm, tpg)

    qc, kc, vc, sgc, qi, ki, kib, wi, kcb, vcb = _odd_proj(x2, mod1, norm_g[1:2], odd_w, tm, tpg)
    oc = _dsa_prompt(seq(qc), seq(qi), seq(wi), seq(kib), seq(kcb), seq(vcb), bias_diag, bias_sub, bias_far)
    x2 = _out_proj(x2, mod1, [(oc.reshape(n, -1), sgc, out_odd_w)], tm, tpg)

    rows = (ka.reshape(1, b, t, N_KV_HEADS, A_HEAD_DIM), va.reshape(1, b, t, N_KV_HEADS, A_HEAD_DIM),
            ckv.reshape(1, b, t, B_KV_RANK), kpe.reshape(1, b, t, B_ROPE),
            kc.reshape(1, b, t, N_KV_HEADS, C_HEAD_DIM), vc.reshape(1, b, t, N_KV_HEADS, C_HEAD_DIM),
            ki.reshape(1, b, t, IDX_DIM))
    return x2.reshape(b, t, D_MODEL), rows


def _pages_copy(pt_ref, seq, first_page, n_pages, hbm_ref, buf_ref, slot, sem, start):
    def body(p, carry):
        cp = pltpu.make_async_copy(hbm_ref.at[pt_ref[seq, first_page + p]], buf_ref.at[slot, p], sem)
        if start:
            cp.start()
        else:
            cp.wait()
        return carry

    lax.fori_loop(0, n_pages, body, 0)


def _stream_chunks(pt_ref, chunk_pages, streams):
    b, c = pl.program_id(0), pl.program_id(1)
    nseq, nch = pl.num_programs(0), pl.num_programs(1)
    step = b * nch + c
    slot = step % 2

    def issue(seq, chunk, sl, start):
        for hbm_ref, buf_ref, sem_ref in streams:
            _pages_copy(pt_ref, seq, chunk * chunk_pages, chunk_pages, hbm_ref, buf_ref, sl, sem_ref.at[sl], start)

    @pl.when(step == 0)
    def _first():
        issue(0, 0, 0, True)

    @pl.when(step + 1 < nseq * nch)
    def _next():
        wrap = c + 1 == nch
        issue(jnp.where(wrap, b + 1, b), jnp.where(wrap, 0, c + 1), 1 - slot, True)

    issue(b, c, slot, False)
    return slot


def _page_row(buf, slot, first_page, n_pages):
    return jnp.concatenate([buf[slot, first_page + p] for p in range(n_pages)], axis=1)


def _new_key_mask(rows, cols, n_new):
    t = lax.broadcasted_iota(jnp.int32, (rows, cols), 0) % 8
    c = lax.broadcasted_iota(jnp.int32, (rows, cols), 1)
    return jnp.logical_and(c < n_new, c <= t)


SROWS = N_HEADS * 8


def _moba_sample_kernel(pt_ref, q_ref, kn_ref, vn_ref, bt_ref, bfar_ref, bn_ref, e_ref, k_hbm, v_hbm, o_ref,
                        kbuf, vbuf, sem, s_ref, *, n_pages, chunk_pages, n_new):
    b = pl.program_id(0)
    nseq = pl.num_programs(0)
    slot = b % 2

    def fetch(seq, sl, start):
        _pages_copy(pt_ref, seq, 0, n_pages, k_hbm, kbuf, sl, sem.at[0, sl], start)
        _pages_copy(pt_ref, seq, 0, n_pages, v_hbm, vbuf, sl, sem.at[1, sl], start)

    @pl.when(b == 0)
    def _first():
        fetch(0, 0, True)

    @pl.when(b + 1 < nseq)
    def _next():
        fetch(b + 1, 1 - slot, True)

    fetch(b, slot, False)

    qs = q_ref[0]
    ck = chunk_pages * PAGE
    nchunk = n_pages // chunk_pages
    nb = e_ref.shape[0]

    def lanes(c):
        return pl.ds(pl.multiple_of(c * ck, ck), ck)

    def score_body(c, gate):
        s = jnp.dot(qs, _page_row(kbuf, slot, c * chunk_pages, chunk_pages).astype(BF16), preferred_element_type=F32)
        s_ref[:, lanes(c)] = s
        e = e_ref[:, lanes(c)]
        hi = s.astype(BF16)
        lo = (s - hi.astype(F32)).astype(BF16)
        return (gate + lax.dot_general(hi, e, _NT, preferred_element_type=F32)
                + lax.dot_general(lo, e, _NT, preferred_element_type=F32))

    gate = lax.fori_loop(0, nchunk, score_body, jnp.zeros((SROWS, nb), F32))
    sel = _top_blocks(gate, nb, MOBA_TOPK).astype(BF16)

    def body(c, carry):
        m, l, acc = carry
        bias = jnp.where(c == nchunk - 1, bt_ref[...], jnp.concatenate([bfar_ref[...]] * (ck // LANE), axis=1))
        keep = jnp.dot(sel, e_ref[:, lanes(c)], preferred_element_type=F32) > 0.5
        s = jnp.where(keep, s_ref[:, lanes(c)] + bias, NEG)
        m, l, alpha, p = _softmax_update(s, m, l)
        vt = _page_row(vbuf, slot, c * chunk_pages, chunk_pages).astype(BF16)
        return m, l, acc * alpha + lax.dot_general(p, vt, _NT, preferred_element_type=F32)

    init = (jnp.full((SROWS, LANE), NEG, F32), jnp.zeros((SROWS, LANE), F32), jnp.zeros((SROWS, LANE), F32))
    m, l, acc = lax.fori_loop(0, nchunk, body, init)
    s = lax.dot_general(qs, kn_ref[0].astype(BF16), _NT, preferred_element_type=F32) + bn_ref[...]
    s = jnp.where(_new_key_mask(SROWS, LANE, n_new), s, NEG)
    m, l, alpha, p = _softmax_update(s, m, l)
    acc = acc * alpha + jnp.dot(p, vn_ref[0].astype(BF16), preferred_element_type=F32)
    o_ref[0] = _normalise(acc, l).astype(o_ref.dtype)


def _moba_sample(page_table, qs, knew, vnew, bias_tail, bias_far, bias_new, expand, cache_k, cache_v, n_new):
    ndb, n_pages = page_table.shape
    chunk_pages = bias_tail.shape[1] // PAGE
    nb = n_pages * PAGE // MOBA_BLOCK
    seq = lambda b, pt: (b, 0, 0)
    const = lambda b, pt: (0, 0)
    grid_spec = pltpu.PrefetchScalarGridSpec(
        num_scalar_prefetch=1,
        grid=(ndb,),
        in_specs=[pl.BlockSpec((1, SROWS, LANE), seq), pl.BlockSpec((1, LANE, LANE), seq),
                  pl.BlockSpec((1, LANE, LANE), seq),
                  pl.BlockSpec(bias_tail.shape, const), pl.BlockSpec(bias_far.shape, const),
                  pl.BlockSpec(bias_new.shape, const), pl.BlockSpec(expand.shape, const),
                  pl.BlockSpec(memory_space=pl.ANY), pl.BlockSpec(memory_space=pl.ANY)],
        out_specs=pl.BlockSpec((1, SROWS, LANE), seq),
        scratch_shapes=[pltpu.VMEM((2, n_pages, LANE, PAGE), F32), pltpu.VMEM((2, n_pages, LANE, PAGE), F32),
                        pltpu.SemaphoreType.DMA((2, 2)), pltpu.VMEM((SROWS, nb * MOBA_BLOCK), F32)])
    return pl.pallas_call(
        functools.partial(_moba_sample_kernel, n_pages=n_pages, chunk_pages=chunk_pages, n_new=n_new),
        out_shape=jax.ShapeDtypeStruct((ndb, SROWS, LANE), BF16),
        grid_spec=grid_spec,
        compiler_params=_cparams(("arbitrary",)),
        name="moba_sample",
    )(page_table, qs, knew, vnew, bias_tail, bias_far, bias_new, expand, cache_k, cache_v)


def _mla_absorb_kernel(q_ref, gk_ref, wuk_ref, o_ref):
    for h in range(N_HEADS):
        blk = slice(h * LANE, (h + 1) * LANE)
        qg = (q_ref[:, blk].astype(F32) * gk_ref[...]).astype(BF16)
        o_ref[:, h * B_KV_RANK:(h + 1) * B_KV_RANK] = lax.dot_general(
            qg, wuk_ref[:, blk], _NT, preferred_element_type=F32).astype(o_ref.dtype)


def _mla_absorb(qb, gk_nope, wuk):
    n = qb.shape[0]
    return pl.pallas_call(
        _mla_absorb_kernel,
        out_shape=jax.ShapeDtypeStruct((n, N_HEADS * B_KV_RANK), BF16),
        name="mla_absorb_q",
    )(qb, gk_nope, wuk)


def _mla_sample_kernel(pt_ref, qa_ref, qp_ref, qps_ref, cn_ref, pn_ref, wuk_ref, rsel_ref, gkp_ref, ct_ref, st_ref,
                       c_hbm, p_hbm, o_ref, cbuf, pbuf, sem, m_ref, l_ref, acc_ref,
                       *, chunk_pages, sub_pages, n_new):
    c = pl.program_id(1)
    nch = pl.num_programs(1)
    slot = _stream_chunks(pt_ref, chunk_pages, [(c_hbm, cbuf, sem.at[0]), (p_hbm, pbuf, sem.at[1])])

    @pl.when(c == 0)
    def _init():
        m_ref[...] = jnp.full(m_ref.shape, NEG, F32)
        l_ref[...] = jnp.zeros(l_ref.shape, F32)
        acc_ref[...] = jnp.zeros(acc_ref.shape, F32)

    qabs, qp, qps = qa_ref[0], qp_ref[0], qps_ref[0]

    def scores(cf, kpt, pos0):
        n = cf.shape[0]
        c16 = cf.astype(BF16)
        kn = jnp.dot(c16, wuk_ref[...], preferred_element_type=F32)
        ssq = lax.dot_general(rsel_ref[...], (kn * kn).astype(BF16), _NT, preferred_element_type=F32)
        ssq = ssq + jnp.sum(kpt * kpt, axis=0, keepdims=True)
        r = lax.rsqrt(ssq * (1.0 / B_QK) + EPS)
        xg = kpt * gkp_ref[...]
        at = pl.ds(pl.multiple_of(pos0, LANE), n)
        xc = (xg * ct_ref[:, at]).astype(BF16)
        xs = (xg * st_ref[:, at]).astype(BF16)
        s = (lax.dot_general(qabs, c16, _NT, preferred_element_type=F32)
             + jnp.dot(qp, xc, preferred_element_type=F32) + jnp.dot(qps, xs, preferred_element_type=F32))
        return s * r, c16

    sub = sub_pages * PAGE
    for u in range(chunk_pages // sub_pages):
        cf = cbuf[slot, pl.ds(u * sub_pages, sub_pages)].reshape(sub, B_KV_RANK)
        kpt = _page_row(pbuf, slot, u * sub_pages, sub_pages)
        s, c16 = scores(cf, kpt, c * (chunk_pages * PAGE) + u * sub)
        _softmax_step(s, c16, m_ref, l_ref, acc_ref)

    @pl.when(c == nch - 1)
    def _new():
        s, c16 = scores(cn_ref[0], pn_ref[0], nch * (chunk_pages * PAGE))
        s = jnp.where(_new_key_mask(SROWS, LANE, n_new), s, NEG)
        _softmax_step(s, c16, m_ref, l_ref, acc_ref)
        o_ref[0] = _normalise(acc_ref[...], l_ref[...])


def _mla_sample(page_table, qabs, qpe, qpes, ckv_new, kpe_new_t, wuk, rsel, gkpe, tables, cache_ckv, cache_kpe_t, n_new):
    ndb, n_pages = page_table.shape
    ct, st = tables
    chunk_pages = min(16, n_pages)
    sub_pages = min(4, chunk_pages)
    nch = n_pages // chunk_pages
    seq = lambda b, c, pt: (b, 0, 0)
    const2 = lambda b, c, pt: (0, 0)
    grid_spec = pltpu.PrefetchScalarGridSpec(
        num_scalar_prefetch=1,
        grid=(ndb, nch),
        in_specs=[pl.BlockSpec((1, SROWS, B_KV_RANK), seq), pl.BlockSpec((1, SROWS, B_ROPE), seq),
                  pl.BlockSpec((1, SROWS, B_ROPE), seq), pl.BlockSpec((1, LANE, B_KV_RANK), seq),
                  pl.BlockSpec((1, B_ROPE, LANE), seq), pl.BlockSpec(wuk.shape, const2), pl.BlockSpec(rsel.shape, const2),
                  pl.BlockSpec(gkpe.shape, const2), pl.BlockSpec(ct.shape, const2), pl.BlockSpec(st.shape, const2),
                  pl.BlockSpec(memory_space=pl.ANY), pl.BlockSpec(memory_space=pl.ANY)],
        out_specs=pl.BlockSpec((1, SROWS, B_KV_RANK), seq),
        scratch_shapes=[pltpu.VMEM((2, chunk_pages, PAGE, B_KV_RANK), F32), pltpu.VMEM((2, chunk_pages, B_ROPE, PAGE), F32),
                        pltpu.SemaphoreType.DMA((2, 2)), pltpu.VMEM((SROWS, LANE), F32), pltpu.VMEM((SROWS, LANE), F32),
                        pltpu.VMEM((SROWS, B_KV_RANK), F32)])
    return pl.pallas_call(
        functools.partial(_mla_sample_kernel, chunk_pages=chunk_pages, sub_pages=sub_pages, n_new=n_new),
        out_shape=jax.ShapeDtypeStruct((ndb, SROWS, B_KV_RANK), F32),
        grid_spec=grid_spec,
        compiler_params=_cparams(("arbitrary", "arbitrary")),
        name="mla_sample",
    )(page_table, qabs, qpe, qpes, ckv_new, kpe_new_t, wuk, rsel, gkpe, ct, st, cache_ckv, cache_kpe_t)


def _mla_value_kernel(o_ref, wuv_ref, out_ref):
    for h in range(N_HEADS):
        out_ref[:, h * LANE:(h + 1) * LANE] = jnp.dot(
            o_ref[:, h * B_KV_RANK:(h + 1) * B_KV_RANK].astype(BF16), wuv_ref[:, h * LANE:(h + 1) * LANE],
            preferred_element_type=F32).astype(out_ref.dtype)


def _mla_value(olat, wuv):
    n = olat.shape[0]
    return pl.pallas_call(
        _mla_value_kernel,
        out_shape=jax.ShapeDtypeStruct((n, 1024), BF16),
        name="mla_value_up",
    )(olat, wuv)


def _mla_rope_tables(n_positions):
    half = B_ROPE // 2
    inv = ROPE_THETA ** (-np.arange(half, dtype=np.float64) / half)
    ang = inv[:, None] * np.arange(n_positions)[None, :]
    ct = np.concatenate([np.cos(ang), np.cos(ang)], axis=0)
    st = np.concatenate([np.sin(ang), -np.sin(ang)], axis=0)
    return jnp.asarray(ct, F32), jnp.asarray(st, F32)


def _dsa_index_kernel(pt_ref, qi_ref, wi_ref, kin_ref, k_hbm, s_ref, sn_ref, kbuf, sem, *, chunk_pages, n_new):
    c = pl.program_id(1)
    slot = _stream_chunks(pt_ref, chunk_pages, [(k_hbm, kbuf, sem)])
    qi = qi_ref[0]
    wi = wi_ref[0]

    def index(keys_t):
        n = keys_t.shape[1]
        sc = jnp.dot(qi, keys_t.astype(BF16), preferred_element_type=F32)
        sc = jnp.maximum(sc, 0.0) * jnp.concatenate([wi] * (n // LANE), axis=1)
        return jnp.sum(sc.reshape(N_HEADS, 8, n), axis=0)

    s_ref[0] = index(_page_row(kbuf, slot, 0, chunk_pages))

    @pl.when(c == pl.num_programs(1) - 1)
    def _new():
        sn_ref[0] = jnp.where(_new_key_mask(8, LANE, n_new), index(kin_ref[0]), NEG)


def _dsa_index(page_table, qis, wis, ki_new, cache_kidx, chunk_pages, n_new):
    ndb, n_pages = page_table.shape
    nch = n_pages // chunk_pages
    ck = chunk_pages * PAGE
    seq = lambda b, c, pt: (b, 0, 0)
    grid_spec = pltpu.PrefetchScalarGridSpec(
        num_scalar_prefetch=1,
        grid=(ndb, nch),
        in_specs=[pl.BlockSpec((1, SROWS, IDX_DIM), seq), pl.BlockSpec((1, SROWS, LANE), seq),
                  pl.BlockSpec((1, IDX_DIM, LANE), seq), pl.BlockSpec(memory_space=pl.ANY)],
        out_specs=[pl.BlockSpec((1, 8, ck), lambda b, c, pt: (b, 0, c)), pl.BlockSpec((1, 8, LANE), seq)],
        scratch_shapes=[pltpu.VMEM((2, chunk_pages, IDX_DIM, PAGE), F32), pltpu.SemaphoreType.DMA((2,))])
    return pl.pallas_call(
        functools.partial(_dsa_index_kernel, chunk_pages=chunk_pages, n_new=n_new),
        out_shape=[jax.ShapeDtypeStruct((ndb, 8, n_pages * PAGE), F32), jax.ShapeDtypeStruct((ndb, 8, LANE), F32)],
        grid_spec=grid_spec,
        compiler_params=_cparams(("arbitrary", "arbitrary")),
        name="dsa_sample_index",
    )(page_table, qis, wis, ki_new, cache_kidx)


def _dsa_select_kernel(s_ref, sn_ref, m_ref, mn_ref, *, topk):
    x, xn = s_ref[0], sn_ref[0]
    sa, sb = _row_stats(x), _row_stats(xn)
    stats = (jnp.maximum(sa[0], sb[0]), jnp.minimum(sa[1], sb[1]), sa[2] + sb[2])
    lo, hi, c_lo, c_hi = _topk_bracket(stats, lambda t: _count_ge(x, t) + _count_ge(xn, t), topk)
    m_ref[0] = x
    mn_ref[0] = xn

    @pl.when(jnp.max(c_lo) > float(topk))
    def _ties():
        tile = MOBA_BLOCK
        need = float(topk) - c_hi
        upper = _strict_upper(tile)

        def fix(j, seen):
            at = pl.ds(pl.multiple_of(j * tile, tile), tile)
            xt, seen = _drop_tied_excess(m_ref[0, :, at], lo, hi, need, seen, upper)
            m_ref[0, :, at] = xt
            return seen

        seen = lax.fori_loop(0, x.shape[1] // tile, fix, jnp.zeros((8, 1), F32))
        mn_ref[0] = _drop_tied_excess(xn, lo, hi, need, seen, _strict_upper(LANE))[0]

    m_ref[0] = jnp.where(m_ref[0] >= lo, 0.0, NEG)
    mn_ref[0] = jnp.where(mn_ref[0] >= lo, 0.0, NEG)


def _dsa_select(scores, scores_new, topk):
    ndb, _, lc = scores.shape
    seq = lambda b: (b, 0, 0)
    return pl.pallas_call(
        functools.partial(_dsa_select_kernel, topk=topk),
        out_shape=[jax.ShapeDtypeStruct(scores.shape, F32), jax.ShapeDtypeStruct(scores_new.shape, F32)],
        grid=(ndb,),
        in_specs=[pl.BlockSpec((1, 8, lc), seq), pl.BlockSpec((1, 8, LANE), seq)],
        out_specs=[pl.BlockSpec((1, 8, lc), seq), pl.BlockSpec((1, 8, LANE), seq)],
        compiler_params=_cparams(("parallel",)),
        name="dsa_sample_select",
    )(scores, scores_new)


def _dsa_sample_kernel(pt_ref, q_ref, mk_ref, mkn_ref, kn_ref, vn_ref, bt_ref, bfar_ref, bn_ref, k_hbm, v_hbm, o_ref,
                       kbuf, vbuf, sem, m_ref, l_ref, acc_ref, *, chunk_pages):
    c = pl.program_id(1)
    nch = pl.num_programs(1)
    slot = _stream_chunks(pt_ref, chunk_pages, [(k_hbm, kbuf, sem.at[0]), (v_hbm, vbuf, sem.at[1])])

    @pl.when(c == 0)
    def _init():
        m_ref[...] = jnp.full(m_ref.shape, NEG, F32)
        l_ref[...] = jnp.zeros(l_ref.shape, F32)
        acc_ref[...] = jnp.zeros(acc_ref.shape, F32)

    half = SROWS // N_KV_HEADS
    ck = chunk_pages * PAGE

    def attend(ks, vs, bias, mask):
        n = ks[0].shape[0]
        for kv in range(N_KV_HEADS):
            rows = slice(kv * half, (kv + 1) * half)
            s = lax.dot_general(q_ref[0, rows, :], ks[kv].astype(BF16), _NT, preferred_element_type=F32)
            s = (s + bias[rows, :]).reshape(GROUP, 8, n) + mask[None]
            _softmax_step(s.reshape(half, n), vs[kv].astype(BF16), m_ref.at[rows, :], l_ref.at[rows, :],
                          acc_ref.at[rows, :])

    def per_head(buf):
        return [buf[slot, :, pl.ds(kv, PAGE, stride=N_KV_HEADS), :].reshape(ck, LANE) for kv in range(N_KV_HEADS)]

    bias = jnp.where(c == nch - 1, bt_ref[...], jnp.concatenate([bfar_ref[...]] * (ck // LANE), axis=1))
    attend(per_head(kbuf), per_head(vbuf), bias, mk_ref[0])

    @pl.when(c == nch - 1)
    def _new():
        kn, vn = kn_ref[0], vn_ref[0]
        attend([kn[:, 0:LANE], kn[:, LANE:2 * LANE]], [vn[:, 0:LANE], vn[:, LANE:2 * LANE]], bn_ref[...], mkn_ref[0])
        o_ref[0] = _normalise(acc_ref[...], l_ref[...]).astype(o_ref.dtype)


def _dsa_sample(page_table, qs, mask, mask_new, knew, vnew, bias_tail, bias_far, bias_new, cache_k, cache_v):
    ndb, n_pages = page_table.shape
    ck = bias_tail.shape[1]
    chunk_pages = ck // PAGE
    nch = n_pages // chunk_pages
    seq = lambda b, c, pt: (b, 0, 0)
    const = lambda b, c, pt: (0, 0)
    grid_spec = pltpu.PrefetchScalarGridSpec(
        num_scalar_prefetch=1,
        grid=(ndb, nch),
        in_specs=[pl.BlockSpec((1, SROWS, LANE), seq), pl.BlockSpec((1, 8, ck), lambda b, c, pt: (b, 0, c)),
                  pl.BlockSpec((1, 8, LANE), seq), pl.BlockSpec((1, LANE, 2 * LANE), seq),
                  pl.BlockSpec((1, LANE, 2 * LANE), seq),
                  pl.BlockSpec(bias_tail.shape, const), pl.BlockSpec(bias_far.shape, const),
                  pl.BlockSpec(bias_new.shape, const),
                  pl.BlockSpec(memory_space=pl.ANY), pl.BlockSpec(memory_space=pl.ANY)],
        out_specs=pl.BlockSpec((1, SROWS, LANE), seq),
        scratch_shapes=[pltpu.VMEM((2, chunk_pages, 2 * PAGE, LANE), F32), pltpu.VMEM((2, chunk_pages, 2 * PAGE, LANE), F32),
                        pltpu.SemaphoreType.DMA((2, 2)), pltpu.VMEM((SROWS, LANE), F32), pltpu.VMEM((SROWS, LANE), F32),
                        pltpu.VMEM((SROWS, LANE), F32)])
    return pl.pallas_call(
        functools.partial(_dsa_sample_kernel, chunk_pages=chunk_pages),
        out_shape=jax.ShapeDtypeStruct((ndb, SROWS, LANE), BF16),
        grid_spec=grid_spec,
        compiler_params=_cparams(("arbitrary", "arbitrary")),
        name="dsa_sample",
    )(page_table, qs, mask, mask_new, knew, vnew, bias_tail, bias_far, bias_new, cache_k, cache_v)


def _stack_heads(a, ndb, t, width):
    a = a.reshape(ndb, t, N_HEADS, width).transpose(0, 2, 1, 3)
    a = jnp.pad(a, ((0, 0), (0, 0), (0, 8 - t), (0, 0)))
    return a.reshape(ndb, SROWS, width)


def _unstack_heads(a, ndb, t):
    width = a.shape[-1]
    a = a.reshape(ndb, N_HEADS, 8, width)[:, :, :t]
    return a.transpose(0, 2, 1, 3).reshape(ndb * t, N_HEADS * width)


def _pad_rows(a, ndb, t, rows):
    a = a.reshape(ndb, t, a.shape[-1])
    return jnp.pad(a, ((0, 0), (0, rows - t), (0, 0)))


def _sample_trunk(x, mods, n_prompt, rel_bias, norm_g, even_w, out_even_w, odd_w, out_odd_w, page_table, caches,
                  gb_k, w_uk):
    ndb, t, _ = x.shape
    n_pages = page_table.shape[1]
    past = n_pages * PAGE
    n = ndb * t
    tm = _token_tile(n)
    x2 = x.reshape(n, D_MODEL)
    cache_mk, cache_mv, cache_ckv, cache_kpe, cache_dk, cache_dv, cache_di = caches
    pos = past + (np.arange(n) % t)
    rope = tuple(tab.reshape(n // tm, tm, LANE) for tab in _rope_tables(pos))

    def row_mod(m):
        m = jnp.repeat(m[n_prompt:n_prompt + ndb], t, axis=0)
        return m.reshape(n // tm, tm, 3 * D_MODEL)

    mod0, mod1 = row_mod(mods[0]), row_mod(mods[1])
    chunk_pages = min(16, n_pages)
    ck = chunk_pages * PAGE
    bias_tail = _bias_tile(rel_bias, 8, ck, ck)
    bias_far = _bias_tile(rel_bias, 8, LANE, past + LANE + FAR_DIST)
    bias_new = _bias_tile(rel_bias, 8, LANE, 0)
    new_t = lambda a: jnp.swapaxes(_pad_rows(a, ndb, t, LANE), 1, 2)

    qa, ka, va, sga, qb, ckv, kpe, kb, vb, sgb = _even_proj(x2, mod0, norm_g[0:1], even_w, rope, 1)
    nb = past // MOBA_BLOCK
    expand = jnp.asarray(np.arange(past)[None, :] // MOBA_BLOCK == np.arange(nb)[:, None], BF16)
    oa = _moba_sample(page_table, _stack_heads(qa, ndb, t, LANE), _pad_rows(ka, ndb, t, LANE),
                      _pad_rows(va, ndb, t, LANE), bias_tail, bias_far, bias_new, expand, cache_mk, cache_mv, t)
    oa = _unstack_heads(oa, ndb, t)
    wuk_full = w_uk.reshape(B_KV_RANK, N_HEADS * B_NOPE).astype(BF16)
    gk_nope = jnp.zeros((1, LANE), F32).at[0, 0:B_NOPE].set(gb_k[0:B_NOPE])
    qabs = _mla_absorb(qb, gk_nope, even_w[1])
    qabs = _stack_heads(qabs, ndb, t, B_KV_RANK)
    qpe = qb.reshape(n, N_HEADS, LANE)[:, :, B_NOPE:B_QK]
    half = B_ROPE // 2
    qpes = jnp.concatenate([qpe[..., half:], qpe[..., :half]], axis=-1)
    qpe = _stack_heads(qpe.reshape(n, -1), ndb, t, B_ROPE)
    qpes = _stack_heads(qpes.reshape(n, -1), ndb, t, B_ROPE)
    rsel = jnp.asarray(np.arange(N_HEADS * B_NOPE)[None, :] // B_NOPE == (np.arange(SROWS) // 8)[:, None], BF16)
    olat = _mla_sample(page_table, qabs, qpe, qpes, _pad_rows(ckv, ndb, t, LANE), new_t(kpe), wuk_full, rsel,
                       gb_k[B_NOPE:B_QK, None], _mla_rope_tables(past + LANE), cache_ckv, cache_kpe, t)
    ob = _mla_value(_unstack_heads(olat, ndb, t), even_w[2])
    wa, wb = out_even_w
    x2 = _out_proj(x2, mod0, [(oa, sga, wa), (ob, sgb, wb)], tm, 1)

    qc, kc, vc, sgc, qi, ki, kib, wi, kcb, vcb = _odd_proj(x2, mod1, norm_g[1:2], odd_w, tm, 1)
    qis = _stack_heads(qi.reshape(n, N_HEADS, LANE)[:, :, 0:IDX_DIM].reshape(n, -1), ndb, t, IDX_DIM)
    wis = jnp.broadcast_to(_stack_heads(wi[:, 0:IDX_HEADS], ndb, t, 1), (ndb, SROWS, LANE))
    scores, scores_new = _dsa_index(page_table, qis, wis, new_t(ki), cache_di, chunk_pages, t)
    mask, mask_new = _dsa_select(scores, scores_new, min(DSA_TOPK, (past + t) // 4))
    oc = _dsa_sample(page_table, _stack_heads(qc, ndb, t, LANE), mask, mask_new, _pad_rows(kc, ndb, t, LANE),
                     _pad_rows(vc, ndb, t, LANE), bias_tail, bias_far, bias_new, cache_dk, cache_dv)
    x2 = _out_proj(x2, mod1, [(_unstack_heads(oc, ndb, t), sgc, out_odd_w)], tm, 1)

    rows = (ka.reshape(1, ndb, t, N_KV_HEADS, A_HEAD_DIM), va.reshape(1, ndb, t, N_KV_HEADS, A_HEAD_DIM),
            ckv.reshape(1, ndb, t, B_KV_RANK), kpe.reshape(1, ndb, t, B_ROPE),
            kc.reshape(1, ndb, t, N_KV_HEADS, C_HEAD_DIM), vc.reshape(1, ndb, t, N_KV_HEADS, C_HEAD_DIM),
            ki.reshape(1, ndb, t, IDX_DIM))
    return x2.reshape(ndb, t, D_MODEL), rows


def kernel(x_prompt, x_sample, cache_moba_k, cache_moba_v, cache_mla_ckv, cache_mla_kpe, cache_dsa_k, cache_dsa_v,
           cache_dsa_kidx, page_table, c_prompt, c_sample, rel_bias, norm_g, ada_w, ada_b, w_in_even, ga_q, ga_k,
           gb_q, gb_k, g_ckv, w_uk, w_uv, w_out_even, w_in_odd, gc_q, gc_k, w_out_odd):
    nb, ndb = x_prompt.shape[0], x_sample.shape[0]
    pad = (-(nb + ndb)) % 8
    c_all = jnp.concatenate([c_prompt, c_sample, jnp.zeros((pad, D_MODEL), F32)], axis=0)
    mods = _modulation(c_all, ada_w, ada_b)
    tq = MOBA_BLOCK
    rel_tiles = (_bias_tile(rel_bias, tq, tq, 0), _bias_tile(rel_bias, tq, tq, tq),
                 _bias_tile(rel_bias, tq, tq, tq + FAR_DIST))
    even_w = _even_weights(w_in_even[0], ga_q[0], ga_k[0], gb_q[0], gb_k[0], g_ckv[0], w_uk[0], w_uv[0])
    out_even_w = _out_weights_even(w_out_even[0])
    odd_w = _odd_weights(w_in_odd[0], gc_q[0], gc_k[0])
    out_odd_w = w_out_odd[0].astype(BF16)
    y_p, rows_p = _prompt_trunk(x_prompt, mods, rel_tiles, norm_g, even_w, out_even_w, odd_w, out_odd_w)
    pool = cache_moba_k.shape[0] * cache_moba_k.shape[1]
    caches = (jnp.transpose(cache_moba_k, (0, 1, 3, 4, 2)).reshape(pool, LANE, PAGE),
              jnp.transpose(cache_moba_v, (0, 1, 3, 4, 2)).reshape(pool, LANE, PAGE),
              cache_mla_ckv.reshape(pool, PAGE, B_KV_RANK),
              jnp.transpose(cache_mla_kpe, (0, 1, 3, 2)).reshape(pool, B_ROPE, PAGE),
              cache_dsa_k.reshape(pool, N_KV_HEADS * PAGE, LANE), cache_dsa_v.reshape(pool, N_KV_HEADS * PAGE, LANE),
              jnp.transpose(cache_dsa_kidx, (0, 1, 3, 2)).reshape(pool, IDX_DIM, PAGE))
    y_s, rows_s = _sample_trunk(x_sample, mods, nb, rel_bias, norm_g, even_w, out_even_w, odd_w, out_odd_w,
                                page_table, caches, gb_k[0], w_uk[0])
    return (y_p, y_s) + rows_p + rows_s
```

```python
import functools
import math

import jax
import jax.numpy as jnp
import numpy as np
from jax import lax
from jax.experimental import pallas as pl
from jax.experimental.pallas import tpu as pltpu

F32 = jnp.float32
BF16 = jnp.bfloat16

D_MODEL = 1024
N_HEADS = 8
N_KV_HEADS = 2
GROUP = N_HEADS // N_KV_HEADS
A_HEAD_DIM = 64
B_NOPE = 64
B_ROPE = 32
B_QK = B_NOPE + B_ROPE
B_VDIM = 64
B_KV_RANK = 256
C_HEAD_DIM = 128
IDX_HEADS = 8
IDX_DIM = 64
MOBA_BLOCK = 256
MOBA_TOPK = 3
DSA_TOPK = 256
PAGE = 128
REL_BUCKETS = 32
REL_MAX_DIST = 128
ROPE_THETA = 10000.0
EPS = 1e-6
LANE = 128
NEG = -1e30
BISECT_ITERS = 40
VMEM_LIMIT = 56 * 1024 * 1024


def _bucket_thresholds():
    n = np.arange(0, 4 * REL_MAX_DIST)
    exact = REL_BUCKETS // 2
    nf = np.maximum(n, 1).astype(np.float64)
    big = exact + (np.log(nf / exact) / math.log(REL_MAX_DIST / exact) * (REL_BUCKETS - exact)).astype(np.int32)
    b = np.where(n < exact, n, np.minimum(big, REL_BUCKETS - 1))
    return [int(np.argmax(b >= k)) for k in range(1, REL_BUCKETS)]


BUCKET_THR = _bucket_thresholds()
FAR_DIST = BUCKET_THR[-1]


def _cparams(sem):
    return pltpu.CompilerParams(dimension_semantics=sem, vmem_limit_bytes=VMEM_LIMIT)


def _bias_kernel(rel_ref, o_ref, *, tq, ncols, offset, shifted):
    t = lax.broadcasted_iota(jnp.int32, (tq, ncols), 0)
    c = lax.broadcasted_iota(jnp.int32, (tq, ncols), 1)
    dist = jnp.maximum(t - c + offset, 0)
    masks = [dist >= th for th in BUCKET_THR]
    for h in range(N_HEADS):
        v = jnp.full((tq, ncols), rel_ref[0, h], F32)
        for k, mk in enumerate(masks):
            v = jnp.where(mk, rel_ref[k + 1, h], v)
        if shifted:
            v = v - rel_ref[REL_BUCKETS - 1, h]
        o_ref[h * tq:(h + 1) * tq, :] = v


def _bias_tile(rel_bias, tq, ncols, offset, shifted=False):
    return pl.pallas_call(
        functools.partial(_bias_kernel, tq=tq, ncols=ncols, offset=offset, shifted=shifted),
        out_shape=jax.ShapeDtypeStruct((N_HEADS * tq, ncols), F32),
        in_specs=[pl.BlockSpec(memory_space=pltpu.SMEM)],
        out_specs=pl.BlockSpec(memory_space=pltpu.VMEM),
        name="t5_bias_tile",
    )(rel_bias)


def _mod_kernel(c_ref, w_ref, b_ref, o_ref):
    c = c_ref[...]
    s = c * jax.nn.sigmoid(c)
    o_ref[0] = jnp.dot(s.astype(BF16), w_ref[0].astype(BF16), preferred_element_type=F32) + b_ref[0]


def _modulation(c_all, ada_w, ada_b):
    depth, _, n3 = ada_w.shape
    r = c_all.shape[0]
    tn = 1024
    return pl.pallas_call(
        _mod_kernel,
        out_shape=jax.ShapeDtypeStruct((depth, r, n3), F32),
        grid=(depth, n3 // tn),
        in_specs=[pl.BlockSpec((r, D_MODEL), lambda l, j: (0, 0)),
                  pl.BlockSpec((1, D_MODEL, tn), lambda l, j: (l, 0, j)),
                  pl.BlockSpec((1, 1, tn), lambda l, j: (l, 0, j))],
        out_specs=pl.BlockSpec((1, r, tn), lambda l, j: (l, 0, j)),
        compiler_params=_cparams(("parallel", "parallel")),
        name="adaln_modulation",
    )(c_all, ada_w, ada_b.reshape(depth, 1, n3))


def _modulated_norm(x, mod, norm_g):
    xn = x * lax.rsqrt(jnp.mean(x * x, axis=-1, keepdims=True) + EPS)
    shift = mod[:, 0:D_MODEL]
    scale = mod[:, D_MODEL:2 * D_MODEL]
    return (xn * norm_g) * (1.0 + scale) + shift


def _silu(x):
    return x * jax.nn.sigmoid(x)


def _block_rms(xh, inv_n):
    ssq = jnp.sum(xh * xh, axis=-1, keepdims=True)
    return xh * lax.rsqrt(ssq * inv_n + EPS)


def _rope_block(xh, c, s1, s2):
    return xh * c + pltpu.roll(xh, LANE - 16, 1) * s1 + pltpu.roll(xh, 16, 1) * s2


def _even_proj_kernel(x_ref, mod_ref, ng_ref, w_ref, wuk_ref, wuv_ref, gaq_ref, gak_ref, gbq_ref, gbk_ref,
                      gckv_ref, rc_ref, rs1_ref, rs2_ref,
                      qa_ref, ka_ref, va_ref, sga_ref, qb_ref, ckv_ref, kpe_ref, kb_ref, vb_ref, sgb_ref):
    h = _modulated_norm(x_ref[...], mod_ref[0], ng_ref[...])
    z = jnp.dot(h.astype(BF16), w_ref[...], preferred_element_type=F32)
    o_qa, o_kv, o_ga, o_qb, o_ckv, o_kpe, o_gb = 0, 1024, 1280, 2304, 3328, 3584, 3712
    rc, rs1, rs2 = rc_ref[0], rs1_ref[0], rs2_ref[0]
    for hd in range(N_HEADS):
        blk = slice(hd * LANE, (hd + 1) * LANE)
        xq = z[:, o_qa + hd * LANE:o_qa + (hd + 1) * LANE]
        qa_ref[:, blk] = (_block_rms(xq, 1.0 / A_HEAD_DIM) * gaq_ref[:, blk] * (A_HEAD_DIM ** -0.5)).astype(qa_ref.dtype)
    ka = z[:, o_kv:o_kv + LANE]
    lane = lax.broadcasted_iota(jnp.int32, ka.shape, 1)
    lo = lane < A_HEAD_DIM
    k2 = ka * ka
    s0 = jnp.sum(jnp.where(lo, k2, 0.0), axis=-1, keepdims=True)
    s1 = jnp.sum(jnp.where(lo, 0.0, k2), axis=-1, keepdims=True)
    rk = lax.rsqrt(jnp.where(lo, s0, s1) * (1.0 / A_HEAD_DIM) + EPS)
    ka_ref[...] = ka * rk * gak_ref[...]
    va_ref[...] = z[:, o_kv + LANE:o_kv + 2 * LANE]
    sga_ref[...] = _silu(z[:, o_ga:o_ga + 1024]).astype(sga_ref.dtype)
    ckv = z[:, o_ckv:o_ckv + B_KV_RANK]
    ckv_n = ckv * lax.rsqrt(jnp.mean(ckv * ckv, axis=-1, keepdims=True) + EPS) * gckv_ref[...]
    ckv_ref[...] = ckv_n
    kpe_blk = z[:, o_kpe:o_kpe + LANE]
    kpe_ref[...] = kpe_blk[:, B_NOPE:B_NOPE + B_ROPE]
    ckv_b = ckv_n.astype(BF16)
    kn = jnp.dot(ckv_b, wuk_ref[...], preferred_element_type=F32)
    vb_ref[...] = jnp.dot(ckv_b, wuv_ref[...], preferred_element_type=F32).astype(vb_ref.dtype)
    for hd in range(N_HEADS):
        blk = slice(hd * LANE, (hd + 1) * LANE)
        xq = z[:, o_qb + hd * LANE:o_qb + (hd + 1) * LANE]
        qn = _block_rms(xq, 1.0 / B_QK) * gbq_ref[...]
        qb_ref[:, blk] = (_rope_block(qn, rc, rs1, rs2) * (B_QK ** -0.5)).astype(qb_ref.dtype)
        xk = kn[:, blk] + kpe_blk
        kk = _block_rms(xk, 1.0 / B_QK) * gbk_ref[...]
        kb_ref[:, blk] = _rope_block(kk, rc, rs1, rs2).astype(kb_ref.dtype)
    sgb_ref[...] = _silu(z[:, o_gb:o_gb + 1024]).astype(sgb_ref.dtype)


def _pad_heads(w, n_heads, width, offset=0):
    k = w.shape[0]
    w = w.reshape(k, n_heads, width)
    out = jnp.zeros((k, n_heads, LANE), w.dtype)
    out = out.at[:, :, offset:offset + width].set(w)
    return out.reshape(k, n_heads * LANE)


def _pad_kv_grouped(w):
    k = w.shape[0]
    w = w.reshape(k, N_KV_HEADS, GROUP, A_HEAD_DIM)
    out = jnp.zeros((k, N_KV_HEADS, GROUP, LANE), w.dtype)
    out = out.at[:, 0, :, 0:A_HEAD_DIM].set(w[:, 0])
    out = out.at[:, 1, :, A_HEAD_DIM:].set(w[:, 1])
    return out.reshape(k, N_HEADS * LANE)


def _rope_tables(positions):
    half = B_ROPE // 2
    inv = ROPE_THETA ** (-np.arange(half, dtype=np.float64) / half)
    ang = np.asarray(positions, dtype=np.float64)[:, None] * inv[None, :]
    cos, sin = np.cos(ang), np.sin(ang)
    n = ang.shape[0]
    c = np.ones((n, LANE)); s1 = np.zeros((n, LANE)); s2 = np.zeros((n, LANE))
    c[:, B_NOPE:B_NOPE + half] = cos
    c[:, B_NOPE + half:B_NOPE + B_ROPE] = cos
    s1[:, B_NOPE:B_NOPE + half] = -sin
    s2[:, B_NOPE + half:B_NOPE + B_ROPE] = sin
    return (jnp.asarray(c, F32), jnp.asarray(s1, F32), jnp.asarray(s2, F32))


def _even_weights(w_in, ga_q, ga_k, gb_q, gb_k, g_ckv, w_uk, w_uv):
    k = D_MODEL
    o = np.cumsum((0, 512, 128, 128, 512, 512, 256, 256, 32, 512))
    seg = [w_in[:, o[i]:o[i + 1]] for i in range(9)]
    w_qa, w_ka, w_va, w_ga, w_qbn, w_qbp, w_ckv, w_kpe, w_gb = seg
    qb = jnp.zeros((k, N_HEADS, LANE), w_in.dtype)
    qb = qb.at[:, :, 0:B_NOPE].set(w_qbn.reshape(k, N_HEADS, B_NOPE))
    qb = qb.at[:, :, B_NOPE:B_QK].set(w_qbp.reshape(k, N_HEADS, B_ROPE))
    kpe = jnp.zeros((k, LANE), w_in.dtype).at[:, B_NOPE:B_QK].set(w_kpe)
    w = jnp.concatenate([_pad_kv_grouped(w_qa), w_ka, w_va, _pad_kv_grouped(w_ga), qb.reshape(k, -1), w_ckv, kpe,
                         _pad_heads(w_gb, N_HEADS, B_VDIM)], axis=1).astype(BF16)
    wuk = _pad_heads(w_uk.reshape(B_KV_RANK, -1), N_HEADS, B_NOPE).astype(BF16)
    wuv = _pad_heads(w_uv.reshape(B_KV_RANK, -1), N_HEADS, B_VDIM).astype(BF16)
    gaq = _pad_kv_grouped(jnp.tile(ga_q, N_HEADS)[None, :])
    gak = jnp.tile(ga_k, N_KV_HEADS)[None, :]
    gbq = jnp.zeros((1, LANE), F32).at[0, 0:B_QK].set(gb_q)
    gbk = jnp.zeros((1, LANE), F32).at[0, 0:B_QK].set(gb_k)
    return w, wuk, wuv, gaq, gak, gbq, gbk, g_ckv[None, :]


def _token_tile(n):
    return 256 if n % 256 == 0 else n


def _even_proj(x, mod, norm_g, wts, rope, tiles_per_group):
    w, wuk, wuv, gaq, gak, gbq, gbk, gckv = wts
    n = x.shape[0]
    tm = rope[0].shape[1]
    ntab = rope[0].shape[0]
    r = mod.shape[1]
    ne = w.shape[1]
    row = lambda i: (i, 0)
    const = lambda i: (0, 0)
    tab = lambda i: (i % ntab, 0, 0)
    outs = [((n, 1024), BF16), ((n, LANE), F32), ((n, LANE), F32), ((n, 1024), BF16), ((n, 1024), BF16),
            ((n, B_KV_RANK), F32), ((n, B_ROPE), F32), ((n, 1024), BF16), ((n, 1024), BF16), ((n, 1024), BF16)]
    return pl.pallas_call(
        _even_proj_kernel,
        out_shape=[jax.ShapeDtypeStruct(s, d) for s, d in outs],
        grid=(n // tm,),
        in_specs=[pl.BlockSpec((tm, D_MODEL), row),
                  pl.BlockSpec((1, r, 3 * D_MODEL), lambda i: (i // tiles_per_group, 0, 0)),
                  pl.BlockSpec((1, D_MODEL), const),
                  pl.BlockSpec((D_MODEL, ne), const),
                  pl.BlockSpec((B_KV_RANK, 1024), const),
                  pl.BlockSpec((B_KV_RANK, 1024), const),
                  pl.BlockSpec((1, 1024), const),
                  pl.BlockSpec((1, LANE), const),
                  pl.BlockSpec((1, LANE), const),
                  pl.BlockSpec((1, LANE), const),
                  pl.BlockSpec((1, B_KV_RANK), const),
                  pl.BlockSpec((1, tm, LANE), tab),
                  pl.BlockSpec((1, tm, LANE), tab),
                  pl.BlockSpec((1, tm, LANE), tab)],
        out_specs=[pl.BlockSpec((tm, s[1]), row) for s, _ in outs],
        compiler_params=_cparams(("parallel",)),
        name="even_proj",
    )(x, mod, norm_g, w, wuk, wuv, gaq, gak, gbq, gbk, gckv, *rope)


def _odd_proj_kernel(x_ref, mod_ref, ng_ref, w_ref, gcq_ref, gck_ref,
                     qc_ref, kc_ref, vc_ref, sgc_ref, qi_ref, ki_ref, kib_ref, wi_ref, kcb_ref, vcb_ref):
    h = _modulated_norm(x_ref[...], mod_ref[0], ng_ref[...])
    z = jnp.dot(h.astype(BF16), w_ref[...], preferred_element_type=F32)
    o_qc, o_kc, o_vc, o_gc, o_qi, o_ki, o_wi = 0, 1024, 1280, 1536, 2560, 3584, 3712
    for hd in range(N_HEADS):
        blk = slice(hd * LANE, (hd + 1) * LANE)
        xq = z[:, o_qc + hd * LANE:o_qc + (hd + 1) * LANE]
        qc_ref[:, blk] = (_block_rms(xq, 1.0 / C_HEAD_DIM) * gcq_ref[...] * (C_HEAD_DIM ** -0.5)).astype(qc_ref.dtype)
    for kv in range(N_KV_HEADS):
        blk = slice(kv * LANE, (kv + 1) * LANE)
        kn = _block_rms(z[:, o_kc + kv * LANE:o_kc + (kv + 1) * LANE], 1.0 / C_HEAD_DIM) * gck_ref[...]
        kc_ref[:, blk] = kn
        kcb_ref[:, blk] = kn.astype(kcb_ref.dtype)
    vc = z[:, o_vc:o_vc + 2 * LANE]
    vc_ref[...] = vc
    vcb_ref[...] = vc.astype(vcb_ref.dtype)
    sgc_ref[...] = _silu(z[:, o_gc:o_gc + 1024]).astype(sgc_ref.dtype)
    qi_ref[...] = (z[:, o_qi:o_qi + 1024] * (IDX_DIM ** -0.5)).astype(qi_ref.dtype)
    ki_blk = z[:, o_ki:o_ki + LANE]
    ki_ref[...] = ki_blk[:, 0:IDX_DIM]
    kib_ref[...] = ki_blk.astype(kib_ref.dtype)
    wi_ref[...] = z[:, o_wi:o_wi + LANE] * (IDX_HEADS ** -0.5)


def _odd_weights(w_in, gc_q, gc_k):
    k = D_MODEL
    o = np.cumsum((0, 1024, 256, 256, 1024, 512, 64, 8))
    w_qc, w_kc, w_vc, w_gc, w_qi, w_ki, w_wi = [w_in[:, o[i]:o[i + 1]] for i in range(7)]
    ki = jnp.zeros((k, LANE), w_in.dtype).at[:, 0:IDX_DIM].set(w_ki)
    wi = jnp.zeros((k, LANE), w_in.dtype).at[:, 0:IDX_HEADS].set(w_wi)
    w = jnp.concatenate([w_qc, w_kc, w_vc, w_gc, _pad_heads(w_qi, IDX_HEADS, IDX_DIM), ki, wi], axis=1).astype(BF16)
    return w, gc_q[None, :], gc_k[None, :]


def _odd_proj(x, mod, norm_g, wts, tm, tiles_per_group):
    w, gcq, gck = wts
    n = x.shape[0]
    r = mod.shape[1]
    no = w.shape[1]
    row = lambda i: (i, 0)
    const = lambda i: (0, 0)
    outs = [((n, 1024), BF16), ((n, 2 * LANE), F32), ((n, 2 * LANE), F32), ((n, 1024), BF16), ((n, 1024), BF16),
            ((n, IDX_DIM), F32), ((n, LANE), BF16), ((n, LANE), F32), ((n, 2 * LANE), BF16), ((n, 2 * LANE), BF16)]
    return pl.pallas_call(
        _odd_proj_kernel,
        out_shape=[jax.ShapeDtypeStruct(s, d) for s, d in outs],
        grid=(n // tm,),
        in_specs=[pl.BlockSpec((tm, D_MODEL), row),
                  pl.BlockSpec((1, r, 3 * D_MODEL), lambda i: (i // tiles_per_group, 0, 0)),
                  pl.BlockSpec((1, D_MODEL), const),
                  pl.BlockSpec((D_MODEL, no), const),
                  pl.BlockSpec((1, LANE), const),
                  pl.BlockSpec((1, LANE), const)],
        out_specs=[pl.BlockSpec((tm, s[1]), row) for s, _ in outs],
        compiler_params=_cparams(("parallel",)),
        name="odd_proj",
    )(x, mod, norm_g, w, gcq, gck)


def _out_proj_kernel(*refs, n_terms):
    x_ref, mod_ref = refs[0], refs[1]
    o_ref = refs[-1]
    acc = None
    for i in range(n_terms):
        a_ref, g_ref, w_ref = refs[2 + 3 * i:5 + 3 * i]
        m = (a_ref[...].astype(F32) * g_ref[...].astype(F32)).astype(BF16)
        d = jnp.dot(m, w_ref[...], preferred_element_type=F32)
        acc = d if acc is None else acc + d
    gate = mod_ref[0][:, 2 * D_MODEL:3 * D_MODEL]
    o_ref[...] = x_ref[...] + gate * acc


def _out_proj(x, mod, terms, tm, tiles_per_group):
    n = x.shape[0]
    r = mod.shape[1]
    row = lambda i: (i, 0)
    const = lambda i: (0, 0)
    in_specs = [pl.BlockSpec((tm, D_MODEL), row),
                pl.BlockSpec((1, r, 3 * D_MODEL), lambda i: (i // tiles_per_group, 0, 0))]
    args = [x, mod]
    for a, g, w in terms:
        in_specs += [pl.BlockSpec((tm, a.shape[1]), row), pl.BlockSpec((tm, g.shape[1]), row),
                     pl.BlockSpec(w.shape, const)]
        args += [a, g, w]
    return pl.pallas_call(
        functools.partial(_out_proj_kernel, n_terms=len(terms)),
        out_shape=jax.ShapeDtypeStruct((n, D_MODEL), F32),
        grid=(n // tm,),
        in_specs=in_specs,
        out_specs=pl.BlockSpec((tm, D_MODEL), row),
        compiler_params=_cparams(("parallel",)),
        name="out_proj",
    )(*args)


def _kmean_kernel(k_ref, o_ref):
    k = k_ref[0]
    nb = k.shape[0] // MOBA_BLOCK
    o_ref[0] = jnp.sum(k.reshape(nb, MOBA_BLOCK, LANE), axis=1) * (1.0 / MOBA_BLOCK)


def _block_means(k):
    b, l, _ = k.shape
    nb = l // MOBA_BLOCK
    return pl.pallas_call(
        _kmean_kernel,
        out_shape=jax.ShapeDtypeStruct((b, nb, LANE), F32),
        grid=(b,),
        in_specs=[pl.BlockSpec((1, l, LANE), lambda i: (i, 0, 0))],
        out_specs=pl.BlockSpec((1, nb, LANE), lambda i: (i, 0, 0)),
        compiler_params=_cparams(("parallel",)),
        name="moba_block_means",
    )(k)


_NT = (((1,), (1,)), ((), ()))


def _top_blocks(gate, n_valid, n_pick):
    nb = gate.shape[1]
    lane = lax.broadcasted_iota(jnp.int32, gate.shape, 1)
    g = jnp.where(lane < n_valid, gate, NEG)
    sel = jnp.zeros(gate.shape, F32)
    for r in range(n_pick):
        mx = jnp.max(g, axis=-1, keepdims=True)
        idx = jnp.min(jnp.where(g == mx, lane, nb), axis=-1, keepdims=True)
        hit = lane == idx
        sel = jnp.where(jnp.logical_and(hit, n_valid > r), 1.0, sel)
        g = jnp.where(hit, NEG, g)
    return sel


def _lane_blocks(x):
    return [x[:, k * LANE:(k + 1) * LANE] for k in range(x.shape[1] // LANE)]


def _softmax_update(s, m_prev, l_prev):
    blocks = _lane_blocks(s)
    m_new = jnp.maximum(m_prev, jnp.max(functools.reduce(jnp.maximum, blocks), axis=-1, keepdims=True))
    alpha = jnp.exp(m_prev - m_new)
    ps = [jnp.exp(b - m_new) for b in blocks]
    l_new = alpha * l_prev + functools.reduce(jnp.add, ps)
    p = ps[0].astype(BF16) if len(ps) == 1 else jnp.concatenate([q.astype(BF16) for q in ps], axis=1)
    return m_new, l_new, alpha, p


def _rescale(acc, alpha):
    reps = acc.shape[1] // LANE
    return acc * (alpha if reps == 1 else jnp.concatenate([alpha] * reps, axis=1))


def _softmax_step(s, v, m_ref, l_ref, acc_ref, nt=False):
    m_new, l_new, alpha, p = _softmax_update(s, m_ref[...], l_ref[...])
    pv = (lax.dot_general(p, v, _NT, preferred_element_type=F32) if nt
          else jnp.dot(p, v, preferred_element_type=F32))
    acc_ref[...] = _rescale(acc_ref[...], alpha) + pv
    m_ref[...] = m_new
    l_ref[...] = l_new


def _normalise(acc, l):
    return acc / jnp.sum(l, axis=-1, keepdims=True)


def _moba_prompt_kernel(ii_ref, jj_ref, last_ref, q_ref, k_ref, v_ref, km_ref, bd_ref, bs_ref, o_ref,
                        qs_ref, sel_ref, m_ref, l_ref, acc_ref):
    step = pl.program_id(1)
    i = ii_ref[step]
    j = jj_ref[step]
    tq = MOBA_BLOCK
    k = k_ref[0].astype(BF16)
    v = v_ref[0].astype(BF16)

    @pl.when(j == i)
    def _own():
        for h in range(N_HEADS):
            qs_ref[h * tq:(h + 1) * tq, :] = q_ref[0, :, h * LANE:(h + 1) * LANE]
        gate = lax.dot_general(qs_ref[...], km_ref[0].astype(BF16), _NT, preferred_element_type=F32)
        sel_ref[...] = _top_blocks(gate, i, MOBA_TOPK).astype(sel_ref.dtype)
        s = lax.dot_general(qs_ref[...], k, _NT, preferred_element_type=F32) + bd_ref[...]
        s3 = s.reshape(N_HEADS, tq, tq)
        row = lax.broadcasted_iota(jnp.int32, (tq, tq), 0)
        col = lax.broadcasted_iota(jnp.int32, (tq, tq), 1)
        s = jnp.where((col <= row)[None], s3, NEG).reshape(N_HEADS * tq, tq)
        m, l, _, p = _softmax_update(s, jnp.full((N_HEADS * tq, LANE), NEG, F32), jnp.zeros((N_HEADS * tq, LANE), F32))
        m_ref[...] = m
        l_ref[...] = l
        acc_ref[...] = jnp.dot(p, v, preferred_element_type=F32)

    @pl.when(j != i)
    def _past():
        nb = sel_ref.shape[1]
        onehot = (lax.broadcasted_iota(jnp.int32, (nb, LANE), 0) == j).astype(BF16)
        chosen = jnp.dot(sel_ref[...], onehot, preferred_element_type=F32) > 0.5
        chosen = jnp.concatenate([chosen] * (tq // LANE), axis=1)
        s = lax.dot_general(qs_ref[...], k, _NT, preferred_element_type=F32)
        s = jnp.where(chosen, s + bs_ref[jnp.where(j == i - 1, 1, 0)], NEG)
        _softmax_step(s, v, m_ref, l_ref, acc_ref)

    @pl.when(last_ref[step] == 1)
    def _fin():
        out = _normalise(acc_ref[...], l_ref[...])
        for h in range(N_HEADS):
            o_ref[0, :, h * LANE:(h + 1) * LANE] = out[h * tq:(h + 1) * tq, :].astype(o_ref.dtype)


def _causal_pairs(nq, own_first):
    ii, jj, last = [], [], []
    for i in range(nq):
        js = ([i] + list(range(i))) if own_first else list(range(i + 1))
        for n, j in enumerate(js):
            ii.append(i); jj.append(j); last.append(1 if n == len(js) - 1 else 0)
    return (jnp.asarray(ii, jnp.int32), jnp.asarray(jj, jnp.int32), jnp.asarray(last, jnp.int32))


def _moba_prompt(qa, ka, va, kmean, bias_diag, bias_sub):
    b, t, _ = qa.shape
    tq = MOBA_BLOCK
    nq = t // tq
    nb = kmean.shape[1]
    ii, jj, last = _causal_pairs(nq, own_first=True)
    r = N_HEADS * tq
    grid_spec = pltpu.PrefetchScalarGridSpec(
        num_scalar_prefetch=3,
        grid=(b, int(ii.shape[0])),
        in_specs=[pl.BlockSpec((1, tq, 1024), lambda bb, s, ii, jj, la: (bb, ii[s], 0)),
                  pl.BlockSpec((1, tq, LANE), lambda bb, s, ii, jj, la: (bb, jj[s], 0)),
                  pl.BlockSpec((1, tq, LANE), lambda bb, s, ii, jj, la: (bb, jj[s], 0)),
                  pl.BlockSpec((1, nb, LANE), lambda bb, s, ii, jj, la: (bb, 0, 0)),
                  pl.BlockSpec((r, tq), lambda bb, s, ii, jj, la: (0, 0)),
                  pl.BlockSpec((2, r, tq), lambda bb, s, ii, jj, la: (0, 0, 0))],
        out_specs=pl.BlockSpec((1, tq, 1024), lambda bb, s, ii, jj, la: (bb, ii[s], 0)),
        scratch_shapes=[pltpu.VMEM((r, LANE), BF16), pltpu.VMEM((r, nb), BF16), pltpu.VMEM((r, LANE), F32),
                        pltpu.VMEM((r, LANE), F32), pltpu.VMEM((r, LANE), F32)])
    return pl.pallas_call(
        _moba_prompt_kernel,
        out_shape=jax.ShapeDtypeStruct((b, t, 1024), BF16),
        grid_spec=grid_spec,
        compiler_params=_cparams(("parallel", "arbitrary")),
        name="moba_prompt",
    )(ii, jj, last, qa, ka, va, kmean, bias_diag, jnp.stack([jnp.zeros_like(bias_sub), bias_sub]))


def _mla_prompt_kernel(ii_ref, jj_ref, last_ref, q_ref, k_ref, v_ref, o_ref, m_ref, l_ref, acc_ref):
    step = pl.program_id(1)
    i = ii_ref[step]
    j = jj_ref[step]
    tq = q_ref.shape[1]

    @pl.when(j == 0)
    def _init():
        m_ref[...] = jnp.full(m_ref.shape, NEG, F32)
        l_ref[...] = jnp.zeros(l_ref.shape, F32)
        acc_ref[...] = jnp.zeros(acc_ref.shape, F32)

    row = lax.broadcasted_iota(jnp.int32, (tq, tq), 0)
    col = lax.broadcasted_iota(jnp.int32, (tq, tq), 1)
    visible = jnp.logical_or(col <= row, j < i)
    for h in range(N_HEADS):
        blk = slice(h * LANE, (h + 1) * LANE)
        s = lax.dot_general(q_ref[0, :, blk], k_ref[0, :, blk], _NT, preferred_element_type=F32)
        s = jnp.where(visible, s, NEG)
        _softmax_step(s, v_ref[0, :, blk], m_ref.at[h], l_ref.at[h], acc_ref.at[:, blk])

    @pl.when(last_ref[step] == 1)
    def _fin():
        for h in range(N_HEADS):
            blk = slice(h * LANE, (h + 1) * LANE)
            o_ref[0, :, blk] = _normalise(acc_ref[:, blk], l_ref[h]).astype(o_ref.dtype)


def _mla_prompt(qb, kb, vb):
    b, t, _ = qb.shape
    tq = 512 if t % 512 == 0 else 256
    nq = t // tq
    ii, jj, last = _causal_pairs(nq, own_first=False)
    grid_spec = pltpu.PrefetchScalarGridSpec(
        num_scalar_prefetch=3,
        grid=(b, int(ii.shape[0])),
        in_specs=[pl.BlockSpec((1, tq, 1024), lambda bb, s, ii, jj, la: (bb, ii[s], 0)),
                  pl.BlockSpec((1, tq, 1024), lambda bb, s, ii, jj, la: (bb, jj[s], 0)),
                  pl.BlockSpec((1, tq, 1024), lambda bb, s, ii, jj, la: (bb, jj[s], 0))],
        out_specs=pl.BlockSpec((1, tq, 1024), lambda bb, s, ii, jj, la: (bb, ii[s], 0)),
        scratch_shapes=[pltpu.VMEM((N_HEADS, tq, LANE), F32), pltpu.VMEM((N_HEADS, tq, LANE), F32),
                        pltpu.VMEM((tq, 1024), F32)])
    return pl.pallas_call(
        _mla_prompt_kernel,
        out_shape=jax.ShapeDtypeStruct((b, t, 1024), BF16),
        grid_spec=grid_spec,
        compiler_params=_cparams(("parallel", "arbitrary")),
        name="mla_prompt",
    )(ii, jj, last, qb, kb, vb)


def _row_stats(x):
    adm = x > 0.5 * NEG
    return (jnp.max(x, axis=-1, keepdims=True), jnp.min(jnp.where(adm, x, -NEG), axis=-1, keepdims=True),
            jnp.sum(jnp.where(adm, 1.0, 0.0), axis=-1, keepdims=True))


def _count_ge(x, t):
    return jnp.sum(jnp.where(x >= t, 1.0, 0.0), axis=-1, keepdims=True)


def _topk_bracket(stats, count_ge, topk):
    mx, mn, n_adm = stats
    kf = float(topk)
    hi0 = mx + jnp.maximum((mx - mn) * (1.0 / 1024.0), jnp.abs(mx) * 1e-6 + 1e-30)

    def cond(state):
        it, _, _, _, _, live = state
        return jnp.logical_and(it < BISECT_ITERS, jnp.max(live) > 0.0)

    def body(state):
        it, lo, hi, c_lo, c_hi, live = state
        mid = lo + (hi - lo) * 0.5
        c = count_ge(mid)
        ge = c >= kf
        c_lo = jnp.where(ge, c, c_lo)
        moved = jnp.logical_and(mid > lo, mid < hi)
        live = jnp.where(jnp.logical_and(moved, c_lo > kf), live, 0.0)
        return it + 1, jnp.where(ge, mid, lo), jnp.where(ge, hi, mid), c_lo, jnp.where(ge, c_hi, c), live

    live0 = jnp.where(jnp.logical_and(n_adm > kf, mx > mn), 1.0, 0.0)
    init = (jnp.int32(0), mn, hi0, n_adm, jnp.zeros_like(mn), live0)
    _, lo, hi, c_lo, c_hi, _ = lax.while_loop(cond, body, init)
    return lo, hi, c_lo, c_hi


def _drop_tied_excess(x, lo, hi, need, seen, upper):
    band = jnp.logical_and(x >= lo, x < hi)
    bf = jnp.where(band, 1.0, 0.0)
    prefix = jnp.dot(bf.astype(BF16), upper, preferred_element_type=F32) + seen
    x = jnp.where(jnp.logical_and(band, prefix >= need), NEG, x)
    return x, seen + jnp.sum(bf, axis=-1, keepdims=True)


def _strict_upper(n):
    return (lax.broadcasted_iota(jnp.int32, (n, n), 0) < lax.broadcasted_iota(jnp.int32, (n, n), 1)).astype(BF16)


def _dsa_prompt_kernel(q_ref, qi_ref, wi_ref, kib_ref, kc_ref, vc_ref, bd_ref, bs_ref, o_ref,
                       qs_ref, qis_ref, wis_ref, s_ref, thr_ref, m_ref, l_ref, acc_ref, *, topk, chunk):
    i = pl.program_id(1)
    tq = q_ref.shape[1]
    half = GROUP * tq
    nlb = tq // LANE
    wi = wi_ref[0]
    for h in range(N_HEADS):
        blk = slice(h * LANE, (h + 1) * LANE)
        qs_ref[h * tq:(h + 1) * tq, :] = q_ref[0, :, blk]
        qis_ref[h * tq:(h + 1) * tq, :] = qi_ref[0, :, blk]
        wis_ref[h * tq:(h + 1) * tq, :] = jnp.broadcast_to(wi[:, h:h + 1], (tq, LANE))
    s_ref[...] = jnp.full(s_ref.shape, NEG, F32)
    m_ref[...] = jnp.full(m_ref.shape, NEG, F32)
    l_ref[...] = jnp.zeros(l_ref.shape, F32)
    acc_ref[...] = jnp.zeros(acc_ref.shape, F32)

    def keys(j):
        return pl.ds(pl.multiple_of(j * tq, tq), tq)

    def index_tile(j):
        sc = lax.dot_general(qis_ref[...], kib_ref[0, keys(j), :], _NT, preferred_element_type=F32)
        sc = jnp.maximum(sc, 0.0) * jnp.concatenate([wis_ref[...]] * nlb, axis=1)
        return jnp.sum(sc.reshape(N_HEADS, tq, tq), axis=0)

    def index_body(j, c):
        s_ref[:, keys(j)] = index_tile(j)
        return c

    lax.fori_loop(0, i, index_body, 0)
    row = lax.broadcasted_iota(jnp.int32, (tq, tq), 0)
    col = lax.broadcasted_iota(jnp.int32, (tq, tq), 1)
    s_ref[:, keys(i)] = jnp.where(col <= row, index_tile(i), NEG)

    n_chunks = ((i + 1) * tq + chunk - 1) // chunk

    def chunk_of(c):
        return s_ref[:, pl.ds(pl.multiple_of(c * chunk, chunk), chunk)]

    def stat_body(c, carry):
        mx, mn, na = _row_stats(chunk_of(c))
        return jnp.maximum(carry[0], mx), jnp.minimum(carry[1], mn), carry[2] + na

    stats = lax.fori_loop(0, n_chunks, stat_body, (jnp.full((tq, 1), NEG, F32), jnp.full((tq, 1), -NEG, F32),
                                                    jnp.zeros((tq, 1), F32)))

    def count_ge(t):
        tb = jnp.broadcast_to(t, (tq, LANE))

        def body(c, a):
            return a + functools.reduce(jnp.add, [jnp.where(b >= tb, 1.0, 0.0) for b in _lane_blocks(chunk_of(c))])

        return jnp.sum(lax.fori_loop(0, n_chunks, body, jnp.zeros((tq, LANE), F32)), axis=-1, keepdims=True)

    lo, hi, c_lo, c_hi = _topk_bracket(stats, count_ge, topk)

    @pl.when(jnp.max(c_lo) > float(topk))
    def _ties():
        upper = _strict_upper(tq)
        need = float(topk) - c_hi

        def fix(j, seen):
            x, seen = _drop_tied_excess(s_ref[:, keys(j)], lo, hi, need, seen, upper)
            s_ref[:, keys(j)] = x
            return seen

        lax.fori_loop(0, i + 1, fix, jnp.zeros((tq, 1), F32))

    thr_ref[...] = jnp.broadcast_to(lo, (tq, LANE))

    def attend(j, bias_ref):
        thr = thr_ref[...]
        keep = jnp.concatenate([b >= thr for b in _lane_blocks(s_ref[:, keys(j)])], axis=1)[None]
        kt = kc_ref[0, keys(j), :]
        vt = vc_ref[0, keys(j), :]
        for kv in range(N_KV_HEADS):
            rows = slice(kv * half, (kv + 1) * half)
            lanes = slice(kv * LANE, (kv + 1) * LANE)
            s = lax.dot_general(qs_ref[rows, :], kt[:, lanes], _NT, preferred_element_type=F32)
            if bias_ref is not None:
                s = s + bias_ref[rows, :]
            s = jnp.where(keep, s.reshape(GROUP, tq, tq), NEG).reshape(half, tq)
            _softmax_step(s, vt[:, lanes], m_ref.at[rows, :], l_ref.at[rows, :], acc_ref.at[rows, :])

    def far_body(j, c):
        attend(j, None)
        return c

    lax.fori_loop(0, jnp.maximum(i - 1, 0), far_body, 0)

    @pl.when(i >= 1)
    def _sub():
        attend(i - 1, bs_ref)

    attend(i, bd_ref)
    out = _normalise(acc_ref[...], l_ref[...])
    for h in range(N_HEADS):
        o_ref[0, :, h * LANE:(h + 1) * LANE] = out[h * tq:(h + 1) * tq, :].astype(o_ref.dtype)


def _dsa_prompt(qc, qi, wi, kib, kcb, vcb, bias_diag, bias_sub):
    b, t, _ = qc.shape
    tq = MOBA_BLOCK
    nq = t // tq
    r = N_HEADS * tq
    topk = min(DSA_TOPK, t // 4)
    chunk = min(1024, t)
    tile = lambda bb, i: (bb, i, 0)
    seq = lambda bb, i: (bb, 0, 0)
    const = lambda bb, i: (0, 0)
    return pl.pallas_call(
        functools.partial(_dsa_prompt_kernel, topk=topk, chunk=chunk),
        out_shape=jax.ShapeDtypeStruct((b, t, 1024), BF16),
        grid=(b, nq),
        in_specs=[pl.BlockSpec((1, tq, 1024), tile), pl.BlockSpec((1, tq, 1024), tile),
                  pl.BlockSpec((1, tq, LANE), tile),
                  pl.BlockSpec((1, t, LANE), seq), pl.BlockSpec((1, t, 2 * LANE), seq),
                  pl.BlockSpec((1, t, 2 * LANE), seq),
                  pl.BlockSpec((r, tq), const), pl.BlockSpec((r, tq), const)],
        out_specs=pl.BlockSpec((1, tq, 1024), tile),
        scratch_shapes=[pltpu.VMEM((r, LANE), BF16), pltpu.VMEM((r, LANE), BF16), pltpu.VMEM((r, LANE), F32),
                        pltpu.VMEM((tq, t), F32), pltpu.VMEM((tq, LANE), F32), pltpu.VMEM((r, LANE), F32),
                        pltpu.VMEM((r, LANE), F32), pltpu.VMEM((r, LANE), F32)],
        compiler_params=_cparams(("parallel", "arbitrary")),
        name="dsa_prompt",
    )(qc, qi, wi, kib, kcb, vcb, bias_diag, bias_sub)


def _out_weights_even(w_out):
    wa = _pad_kv_grouped(w_out[0:N_HEADS * A_HEAD_DIM].T).T.astype(BF16)
    wb = _pad_heads(w_out[N_HEADS * A_HEAD_DIM:].T, N_HEADS, B_VDIM).T.astype(BF16)
    return wa, wb


def _prompt_trunk(x, mods, rel_tiles, norm_g, even_w, out_even_w, odd_w, out_odd_w):
    b, t, _ = x.shape
    n = b * t
    tm = _token_tile(n)
    tpg = t // tm
    x2 = x.reshape(n, D_MODEL)
    bias_diag, bias_sub = rel_tiles
    rope = tuple(tab.reshape(t // tm, tm, LANE) for tab in _rope_tables(np.arange(t)))
    mod0 = mods[0][:b].reshape(b, 1, 3 * D_MODEL)
    mod1 = mods[1][:b].reshape(b, 1, 3 * D_MODEL)

    qa, ka, va, sga, qb, ckv, kpe, kb, vb, sgb = _even_proj(x2, mod0, norm_g[0:1], even_w, rope, tpg)
    seq = lambda a: a.reshape(b, t, a.shape[-1])
    kmean = _block_means(seq(ka))
    oa = _moba_prompt(seq(qa), seq(ka), seq(va), kmean, bias_diag, bias_sub)
    ob = _mla_prompt(seq(qb), seq(kb), seq(vb))
    wa, wb = out_even_w
    x2 = _out_proj(x2, mod0, [(oa.reshape(n, -1), sga, wa), (ob.reshape(n, -1), sgb, wb)], tm, tpg)

    qc, kc, vc, sgc, qi, ki, kib, wi, kcb, vcb = _odd_proj(x2, mod1, norm_g[1:2], odd_w, tm, tpg)
    oc = _dsa_prompt(seq(qc), seq(qi), seq(wi), seq(kib), seq(kcb), seq(vcb), bias_diag, bias_sub)
    x2 = _out_proj(x2, mod1, [(oc.reshape(n, -1), sgc, out_odd_w)], tm, tpg)

    rows = (ka.reshape(1, b, t, N_KV_HEADS, A_HEAD_DIM), va.reshape(1, b, t, N_KV_HEADS, A_HEAD_DIM),
            ckv.reshape(1, b, t, B_KV_RANK), kpe.reshape(1, b, t, B_ROPE),
            kc.reshape(1, b, t, N_KV_HEADS, C_HEAD_DIM), vc.reshape(1, b, t, N_KV_HEADS, C_HEAD_DIM),
            ki.reshape(1, b, t, IDX_DIM))
    return x2.reshape(b, t, D_MODEL), rows


def _pages_copy(pt_ref, seq, first_page, n_pages, hbm_ref, buf_ref, slot, sem, start):
    def body(p, carry):
        cp = pltpu.make_async_copy(hbm_ref.at[pt_ref[seq, first_page + p]], buf_ref.at[slot, p], sem)
        if start:
            cp.start()
        else:
            cp.wait()
        return carry

    lax.fori_loop(0, n_pages, body, 0)


def _stream_chunks(pt_ref, chunk_pages, streams):
    b, c = pl.program_id(0), pl.program_id(1)
    nseq, nch = pl.num_programs(0), pl.num_programs(1)
    step = b * nch + c
    slot = step % 2

    def issue(seq, chunk, sl, start):
        for hbm_ref, buf_ref, sem_ref in streams:
            _pages_copy(pt_ref, seq, chunk * chunk_pages, chunk_pages, hbm_ref, buf_ref, sl, sem_ref.at[sl], start)

    @pl.when(step == 0)
    def _first():
        issue(0, 0, 0, True)

    @pl.when(step + 1 < nseq * nch)
    def _next():
        wrap = c + 1 == nch
        issue(jnp.where(wrap, b + 1, b), jnp.where(wrap, 0, c + 1), 1 - slot, True)

    issue(b, c, slot, False)
    return slot


def _page_row(buf, slot, first_page, n_pages):
    return jnp.concatenate([buf[slot, first_page + p] for p in range(n_pages)], axis=1)


def _new_key_mask(rows, cols, n_new):
    t = lax.broadcasted_iota(jnp.int32, (rows, cols), 0) % 8
    c = lax.broadcasted_iota(jnp.int32, (rows, cols), 1)
    return jnp.logical_and(c < n_new, c <= t)


SROWS = N_HEADS * 8


def _moba_sample_kernel(pt_ref, q_ref, kn_ref, vn_ref, bt_ref, bfar_ref, bn_ref, e_ref, k_hbm, v_hbm, o_ref,
                        kbuf, vbuf, sem, s_ref, *, n_pages, chunk_pages, n_new):
    b = pl.program_id(0)
    nseq = pl.num_programs(0)
    slot = b % 2

    def fetch(seq, sl, start):
        _pages_copy(pt_ref, seq, 0, n_pages, k_hbm, kbuf, sl, sem.at[0, sl], start)
        _pages_copy(pt_ref, seq, 0, n_pages, v_hbm, vbuf, sl, sem.at[1, sl], start)

    @pl.when(b == 0)
    def _first():
        fetch(0, 0, True)

    @pl.when(b + 1 < nseq)
    def _next():
        fetch(b + 1, 1 - slot, True)

    fetch(b, slot, False)

    qs = q_ref[0]
    ck = chunk_pages * PAGE
    nchunk = n_pages // chunk_pages
    nb = e_ref.shape[0]

    def lanes(c):
        return pl.ds(pl.multiple_of(c * ck, ck), ck)

    def score_body(c, gate):
        s = jnp.dot(qs, _page_row(kbuf, slot, c * chunk_pages, chunk_pages).astype(BF16), preferred_element_type=F32)
        s_ref[:, lanes(c)] = s
        e = e_ref[:, lanes(c)]
        hi = s.astype(BF16)
        lo = (s - hi.astype(F32)).astype(BF16)
        return (gate + lax.dot_general(hi, e, _NT, preferred_element_type=F32)
                + lax.dot_general(lo, e, _NT, preferred_element_type=F32))

    gate = lax.fori_loop(0, nchunk, score_body, jnp.zeros((SROWS, nb), F32))
    sel = _top_blocks(gate, nb, MOBA_TOPK).astype(BF16)

    def body(c, carry):
        m, l, acc = carry
        bias = jnp.where(c == nchunk - 1, bt_ref[...], jnp.concatenate([bfar_ref[...]] * (ck // LANE), axis=1))
        keep = jnp.dot(sel, e_ref[:, lanes(c)], preferred_element_type=F32) > 0.5
        s = jnp.where(keep, s_ref[:, lanes(c)] + bias, NEG)
        m, l, alpha, p = _softmax_update(s, m, l)
        vt = _page_row(vbuf, slot, c * chunk_pages, chunk_pages).astype(BF16)
        return m, l, acc * alpha + lax.dot_general(p, vt, _NT, preferred_element_type=F32)

    init = (jnp.full((SROWS, LANE), NEG, F32), jnp.zeros((SROWS, LANE), F32), jnp.zeros((SROWS, LANE), F32))
    m, l, acc = lax.fori_loop(0, nchunk, body, init)
    s = lax.dot_general(qs, kn_ref[0].astype(BF16), _NT, preferred_element_type=F32) + bn_ref[...]
    s = jnp.where(_new_key_mask(SROWS, LANE, n_new), s, NEG)
    m, l, alpha, p = _softmax_update(s, m, l)
    acc = acc * alpha + jnp.dot(p, vn_ref[0].astype(BF16), preferred_element_type=F32)
    o_ref[0] = _normalise(acc, l).astype(o_ref.dtype)


def _moba_sample(page_table, qs, knew, vnew, bias_tail, bias_far, bias_new, expand, cache_k, cache_v, n_new):
    ndb, n_pages = page_table.shape
    chunk_pages = bias_tail.shape[1] // PAGE
    nb = n_pages * PAGE // MOBA_BLOCK
    seq = lambda b, pt: (b, 0, 0)
    const = lambda b, pt: (0, 0)
    grid_spec = pltpu.PrefetchScalarGridSpec(
        num_scalar_prefetch=1,
        grid=(ndb,),
        in_specs=[pl.BlockSpec((1, SROWS, LANE), seq), pl.BlockSpec((1, LANE, LANE), seq),
                  pl.BlockSpec((1, LANE, LANE), seq),
                  pl.BlockSpec(bias_tail.shape, const), pl.BlockSpec(bias_far.shape, const),
                  pl.BlockSpec(bias_new.shape, const), pl.BlockSpec(expand.shape, const),
                  pl.BlockSpec(memory_space=pl.ANY), pl.BlockSpec(memory_space=pl.ANY)],
        out_specs=pl.BlockSpec((1, SROWS, LANE), seq),
        scratch_shapes=[pltpu.VMEM((2, n_pages, LANE, PAGE), F32), pltpu.VMEM((2, n_pages, LANE, PAGE), F32),
                        pltpu.SemaphoreType.DMA((2, 2)), pltpu.VMEM((SROWS, nb * MOBA_BLOCK), F32)])
    return pl.pallas_call(
        functools.partial(_moba_sample_kernel, n_pages=n_pages, chunk_pages=chunk_pages, n_new=n_new),
        out_shape=jax.ShapeDtypeStruct((ndb, SROWS, LANE), BF16),
        grid_spec=grid_spec,
        compiler_params=_cparams(("arbitrary",)),
        name="moba_sample",
    )(page_table, qs, knew, vnew, bias_tail, bias_far, bias_new, expand, cache_k, cache_v)


def _mla_absorb_kernel(q_ref, gk_ref, wuk_ref, o_ref):
    for h in range(N_HEADS):
        blk = slice(h * LANE, (h + 1) * LANE)
        qg = (q_ref[:, blk].astype(F32) * gk_ref[...]).astype(BF16)
        o_ref[:, h * B_KV_RANK:(h + 1) * B_KV_RANK] = lax.dot_general(
            qg, wuk_ref[:, blk], _NT, preferred_element_type=F32).astype(o_ref.dtype)


def _mla_absorb(qb, gk_nope, wuk):
    n = qb.shape[0]
    return pl.pallas_call(
        _mla_absorb_kernel,
        out_shape=jax.ShapeDtypeStruct((n, N_HEADS * B_KV_RANK), BF16),
        name="mla_absorb_q",
    )(qb, gk_nope, wuk)


def _mla_sample_kernel(pt_ref, qa_ref, qp_ref, qps_ref, cn_ref, pn_ref, wuk_ref, rsel_ref, gkp_ref, ct_ref, st_ref,
                       c_hbm, p_hbm, o_ref, cbuf, pbuf, sem, m_ref, l_ref, acc_ref,
                       *, chunk_pages, sub_pages, n_new):
    c = pl.program_id(1)
    nch = pl.num_programs(1)
    slot = _stream_chunks(pt_ref, chunk_pages, [(c_hbm, cbuf, sem.at[0]), (p_hbm, pbuf, sem.at[1])])

    @pl.when(c == 0)
    def _init():
        m_ref[...] = jnp.full(m_ref.shape, NEG, F32)
        l_ref[...] = jnp.zeros(l_ref.shape, F32)
        acc_ref[...] = jnp.zeros(acc_ref.shape, F32)

    qabs, qp, qps = qa_ref[0], qp_ref[0], qps_ref[0]

    def scores(cf, kpt, pos0):
        n = cf.shape[0]
        c16 = cf.astype(BF16)
        kn = jnp.dot(c16, wuk_ref[...], preferred_element_type=F32)
        ssq = lax.dot_general(rsel_ref[...], (kn * kn).astype(BF16), _NT, preferred_element_type=F32)
        ssq = ssq + jnp.sum(kpt * kpt, axis=0, keepdims=True)
        r = lax.rsqrt(ssq * (1.0 / B_QK) + EPS)
        xg = kpt * gkp_ref[...]
        at = pl.ds(pl.multiple_of(pos0, LANE), n)
        xc = (xg * ct_ref[:, at]).astype(BF16)
        xs = (xg * st_ref[:, at]).astype(BF16)
        s = (lax.dot_general(qabs, c16, _NT, preferred_element_type=F32)
             + jnp.dot(qp, xc, preferred_element_type=F32) + jnp.dot(qps, xs, preferred_element_type=F32))
        return s * r, c16

    sub = sub_pages * PAGE
    for u in range(chunk_pages // sub_pages):
        cf = cbuf[slot, pl.ds(u * sub_pages, sub_pages)].reshape(sub, B_KV_RANK)
        kpt = _page_row(pbuf, slot, u * sub_pages, sub_pages)
        s, c16 = scores(cf, kpt, c * (chunk_pages * PAGE) + u * sub)
        _softmax_step(s, c16, m_ref, l_ref, acc_ref)

    @pl.when(c == nch - 1)
    def _new():
        s, c16 = scores(cn_ref[0], pn_ref[0], nch * (chunk_pages * PAGE))
        s = jnp.where(_new_key_mask(SROWS, LANE, n_new), s, NEG)
        _softmax_step(s, c16, m_ref, l_ref, acc_ref)
        o_ref[0] = _normalise(acc_ref[...], l_ref[...])


def _mla_sample(page_table, qabs, qpe, qpes, ckv_new, kpe_new_t, wuk, rsel, gkpe, tables, cache_ckv, cache_kpe_t, n_new):
    ndb, n_pages = page_table.shape
    ct, st = tables
    chunk_pages = min(16, n_pages)
    sub_pages = chunk_pages
    nch = n_pages // chunk_pages
    seq = lambda b, c, pt: (b, 0, 0)
    const2 = lambda b, c, pt: (0, 0)
    grid_spec = pltpu.PrefetchScalarGridSpec(
        num_scalar_prefetch=1,
        grid=(ndb, nch),
        in_specs=[pl.BlockSpec((1, SROWS, B_KV_RANK), seq), pl.BlockSpec((1, SROWS, B_ROPE), seq),
                  pl.BlockSpec((1, SROWS, B_ROPE), seq), pl.BlockSpec((1, LANE, B_KV_RANK), seq),
                  pl.BlockSpec((1, B_ROPE, LANE), seq), pl.BlockSpec(wuk.shape, const2), pl.BlockSpec(rsel.shape, const2),
                  pl.BlockSpec(gkpe.shape, const2), pl.BlockSpec(ct.shape, const2), pl.BlockSpec(st.shape, const2),
                  pl.BlockSpec(memory_space=pl.ANY), pl.BlockSpec(memory_space=pl.ANY)],
        out_specs=pl.BlockSpec((1, SROWS, B_KV_RANK), seq),
        scratch_shapes=[pltpu.VMEM((2, chunk_pages, PAGE, B_KV_RANK), F32), pltpu.VMEM((2, chunk_pages, B_ROPE, PAGE), F32),
                        pltpu.SemaphoreType.DMA((2, 2)), pltpu.VMEM((SROWS, LANE), F32), pltpu.VMEM((SROWS, LANE), F32),
                        pltpu.VMEM((SROWS, B_KV_RANK), F32)])
    return pl.pallas_call(
        functools.partial(_mla_sample_kernel, chunk_pages=chunk_pages, sub_pages=sub_pages, n_new=n_new),
        out_shape=jax.ShapeDtypeStruct((ndb, SROWS, B_KV_RANK), F32),
        grid_spec=grid_spec,
        compiler_params=_cparams(("arbitrary", "arbitrary")),
        name="mla_sample",
    )(page_table, qabs, qpe, qpes, ckv_new, kpe_new_t, wuk, rsel, gkpe, ct, st, cache_ckv, cache_kpe_t)


def _mla_value_kernel(o_ref, wuv_ref, out_ref):
    for h in range(N_HEADS):
        out_ref[:, h * LANE:(h + 1) * LANE] = jnp.dot(
            o_ref[:, h * B_KV_RANK:(h + 1) * B_KV_RANK].astype(BF16), wuv_ref[:, h * LANE:(h + 1) * LANE],
            preferred_element_type=F32).astype(out_ref.dtype)


def _mla_value(olat, wuv):
    n = olat.shape[0]
    return pl.pallas_call(
        _mla_value_kernel,
        out_shape=jax.ShapeDtypeStruct((n, 1024), BF16),
        name="mla_value_up",
    )(olat, wuv)


def _mla_rope_tables(n_positions):
    half = B_ROPE // 2
    inv = ROPE_THETA ** (-np.arange(half, dtype=np.float64) / half)
    ang = inv[:, None] * np.arange(n_positions)[None, :]
    ct = np.concatenate([np.cos(ang), np.cos(ang)], axis=0)
    st = np.concatenate([np.sin(ang), -np.sin(ang)], axis=0)
    return jnp.asarray(ct, F32), jnp.asarray(st, F32)


def _dsa_index_kernel(pt_ref, qi_ref, wi_ref, kin_ref, k_hbm, s_ref, sn_ref, kbuf, sem, *, chunk_pages, n_new):
    c = pl.program_id(1)
    slot = _stream_chunks(pt_ref, chunk_pages, [(k_hbm, kbuf, sem)])
    qi = qi_ref[0]
    wi = wi_ref[0]

    def index(keys_t):
        n = keys_t.shape[1]
        sc = jnp.dot(qi, keys_t.astype(BF16), preferred_element_type=F32)
        sc = jnp.maximum(sc, 0.0) * jnp.concatenate([wi] * (n // LANE), axis=1)
        return jnp.sum(sc.reshape(N_HEADS, 8, n), axis=0)

    s_ref[0] = index(_page_row(kbuf, slot, 0, chunk_pages))

    @pl.when(c == pl.num_programs(1) - 1)
    def _new():
        sn_ref[0] = jnp.where(_new_key_mask(8, LANE, n_new), index(kin_ref[0]), NEG)


def _dsa_index(page_table, qis, wis, ki_new, cache_kidx, chunk_pages, n_new):
    ndb, n_pages = page_table.shape
    nch = n_pages // chunk_pages
    ck = chunk_pages * PAGE
    seq = lambda b, c, pt: (b, 0, 0)
    grid_spec = pltpu.PrefetchScalarGridSpec(
        num_scalar_prefetch=1,
        grid=(ndb, nch),
        in_specs=[pl.BlockSpec((1, SROWS, IDX_DIM), seq), pl.BlockSpec((1, SROWS, LANE), seq),
                  pl.BlockSpec((1, IDX_DIM, LANE), seq), pl.BlockSpec(memory_space=pl.ANY)],
        out_specs=[pl.BlockSpec((1, 8, ck), lambda b, c, pt: (b, 0, c)), pl.BlockSpec((1, 8, LANE), seq)],
        scratch_shapes=[pltpu.VMEM((2, chunk_pages, IDX_DIM, PAGE), F32), pltpu.SemaphoreType.DMA((2,))])
    return pl.pallas_call(
        functools.partial(_dsa_index_kernel, chunk_pages=chunk_pages, n_new=n_new),
        out_shape=[jax.ShapeDtypeStruct((ndb, 8, n_pages * PAGE), F32), jax.ShapeDtypeStruct((ndb, 8, LANE), F32)],
        grid_spec=grid_spec,
        compiler_params=_cparams(("arbitrary", "arbitrary")),
        name="dsa_sample_index",
    )(page_table, qis, wis, ki_new, cache_kidx)


def _dsa_select_kernel(s_ref, sn_ref, m_ref, mn_ref, *, topk, n_new):
    x, xn = s_ref[0], sn_ref[0]
    sa, sb = _row_stats(x), _row_stats(xn)
    stats = (jnp.maximum(sa[0], sb[0]), jnp.minimum(sa[1], sb[1]), sa[2] + sb[2])
    real = lax.broadcasted_iota(jnp.int32, (8, 1), 0) < n_new
    stats = (stats[0], stats[1], jnp.where(real, stats[2], 0.0))
    lo, hi, c_lo, c_hi = _topk_bracket(stats, lambda t: _count_ge(x, t) + _count_ge(xn, t), topk)
    m_ref[0] = x
    mn_ref[0] = xn

    @pl.when(jnp.max(c_lo) > float(topk))
    def _ties():
        tile = MOBA_BLOCK
        need = float(topk) - c_hi
        upper = _strict_upper(tile)

        def fix(j, seen):
            at = pl.ds(pl.multiple_of(j * tile, tile), tile)
            xt, seen = _drop_tied_excess(m_ref[0, :, at], lo, hi, need, seen, upper)
            m_ref[0, :, at] = xt
            return seen

        seen = lax.fori_loop(0, x.shape[1] // tile, fix, jnp.zeros((8, 1), F32))
        mn_ref[0] = _drop_tied_excess(xn, lo, hi, need, seen, _strict_upper(LANE))[0]

    m_ref[0] = jnp.where(m_ref[0] >= lo, 0.0, NEG)
    mn_ref[0] = jnp.where(mn_ref[0] >= lo, 0.0, NEG)


def _dsa_select(scores, scores_new, topk, n_new):
    ndb, _, lc = scores.shape
    seq = lambda b: (b, 0, 0)
    return pl.pallas_call(
        functools.partial(_dsa_select_kernel, topk=topk, n_new=n_new),
        out_shape=[jax.ShapeDtypeStruct(scores.shape, F32), jax.ShapeDtypeStruct(scores_new.shape, F32)],
        grid=(ndb,),
        in_specs=[pl.BlockSpec((1, 8, lc), seq), pl.BlockSpec((1, 8, LANE), seq)],
        out_specs=[pl.BlockSpec((1, 8, lc), seq), pl.BlockSpec((1, 8, LANE), seq)],
        compiler_params=_cparams(("parallel",)),
        name="dsa_sample_select",
    )(scores, scores_new)


def _dsa_sample_kernel(pt_ref, q_ref, mk_ref, mkn_ref, kn_ref, vn_ref, bt_ref, bfar_ref, bn_ref, k_hbm, v_hbm, o_ref,
                       kbuf, vbuf, sem, m_ref, l_ref, acc_ref, *, chunk_pages):
    c = pl.program_id(1)
    nch = pl.num_programs(1)
    slot = _stream_chunks(pt_ref, chunk_pages, [(k_hbm, kbuf, sem.at[0]), (v_hbm, vbuf, sem.at[1])])

    @pl.when(c == 0)
    def _init():
        m_ref[...] = jnp.full(m_ref.shape, NEG, F32)
        l_ref[...] = jnp.zeros(l_ref.shape, F32)
        acc_ref[...] = jnp.zeros(acc_ref.shape, F32)

    half = SROWS // N_KV_HEADS
    ck = chunk_pages * PAGE

    def attend(ks, vs, bias, mask):
        n = ks[0].shape[0]
        for kv in range(N_KV_HEADS):
            rows = slice(kv * half, (kv + 1) * half)
            s = lax.dot_general(q_ref[0, rows, :], ks[kv].astype(BF16), _NT, preferred_element_type=F32)
            s = (s + bias[rows, :]).reshape(GROUP, 8, n) + mask[None]
            _softmax_step(s.reshape(half, n), vs[kv].astype(BF16), m_ref.at[rows, :], l_ref.at[rows, :],
                          acc_ref.at[rows, :])

    def per_head(buf):
        return [buf[slot, :, pl.ds(kv, PAGE, stride=N_KV_HEADS), :].reshape(ck, LANE) for kv in range(N_KV_HEADS)]

    bias = jnp.where(c == nch - 1, bt_ref[...], jnp.concatenate([bfar_ref[...]] * (ck // LANE), axis=1))
    attend(per_head(kbuf), per_head(vbuf), bias, mk_ref[0])

    @pl.when(c == nch - 1)
    def _new():
        kn, vn = kn_ref[0], vn_ref[0]
        attend([kn[:, 0:LANE], kn[:, LANE:2 * LANE]], [vn[:, 0:LANE], vn[:, LANE:2 * LANE]], bn_ref[...], mkn_ref[0])
        o_ref[0] = _normalise(acc_ref[...], l_ref[...]).astype(o_ref.dtype)


def _dsa_sample(page_table, qs, mask, mask_new, knew, vnew, bias_tail, bias_far, bias_new, cache_k, cache_v):
    ndb, n_pages = page_table.shape
    ck = bias_tail.shape[1]
    chunk_pages = ck // PAGE
    nch = n_pages // chunk_pages
    seq = lambda b, c, pt: (b, 0, 0)
    const = lambda b, c, pt: (0, 0)
    grid_spec = pltpu.PrefetchScalarGridSpec(
        num_scalar_prefetch=1,
        grid=(ndb, nch),
        in_specs=[pl.BlockSpec((1, SROWS, LANE), seq), pl.BlockSpec((1, 8, ck), lambda b, c, pt: (b, 0, c)),
                  pl.BlockSpec((1, 8, LANE), seq), pl.BlockSpec((1, LANE, 2 * LANE), seq),
                  pl.BlockSpec((1, LANE, 2 * LANE), seq),
                  pl.BlockSpec(bias_tail.shape, const), pl.BlockSpec(bias_far.shape, const),
                  pl.BlockSpec(bias_new.shape, const),
                  pl.BlockSpec(memory_space=pl.ANY), pl.BlockSpec(memory_space=pl.ANY)],
        out_specs=pl.BlockSpec((1, SROWS, LANE), seq),
        scratch_shapes=[pltpu.VMEM((2, chunk_pages, 2 * PAGE, LANE), F32), pltpu.VMEM((2, chunk_pages, 2 * PAGE, LANE), F32),
                        pltpu.SemaphoreType.DMA((2, 2)), pltpu.VMEM((SROWS, LANE), F32), pltpu.VMEM((SROWS, LANE), F32),
                        pltpu.VMEM((SROWS, LANE), F32)])
    return pl.pallas_call(
        functools.partial(_dsa_sample_kernel, chunk_pages=chunk_pages),
        out_shape=jax.ShapeDtypeStruct((ndb, SROWS, LANE), BF16),
        grid_spec=grid_spec,
        compiler_params=_cparams(("arbitrary", "arbitrary")),
        name="dsa_sample",
    )(page_table, qs, mask, mask_new, knew, vnew, bias_tail, bias_far, bias_new, cache_k, cache_v)


def _stack_heads(a, ndb, t, width):
    a = a.reshape(ndb, t, N_HEADS, width).transpose(0, 2, 1, 3)
    a = jnp.pad(a, ((0, 0), (0, 0), (0, 8 - t), (0, 0)))
    return a.reshape(ndb, SROWS, width)


def _unstack_heads(a, ndb, t):
    width = a.shape[-1]
    a = a.reshape(ndb, N_HEADS, 8, width)[:, :, :t]
    return a.transpose(0, 2, 1, 3).reshape(ndb * t, N_HEADS * width)


def _pad_rows(a, ndb, t, rows):
    a = a.reshape(ndb, t, a.shape[-1])
    return jnp.pad(a, ((0, 0), (0, rows - t), (0, 0)))


def _sample_trunk(x, mods, n_prompt, rel_bias, norm_g, even_w, out_even_w, odd_w, out_odd_w, page_table, caches,
                  gb_k, w_uk):
    ndb, t, _ = x.shape
    n_pages = page_table.shape[1]
    past = n_pages * PAGE
    n = ndb * t
    tm = _token_tile(n)
    x2 = x.reshape(n, D_MODEL)
    cache_mk, cache_mv, cache_ckv, cache_kpe, cache_dk, cache_dv, cache_di = caches
    pos = past + (np.arange(n) % t)
    rope = tuple(tab.reshape(n // tm, tm, LANE) for tab in _rope_tables(pos))

    def row_mod(m):
        m = jnp.repeat(m[n_prompt:n_prompt + ndb], t, axis=0)
        return m.reshape(n // tm, tm, 3 * D_MODEL)

    mod0, mod1 = row_mod(mods[0]), row_mod(mods[1])
    chunk_pages = min(16, n_pages)
    ck = chunk_pages * PAGE
    bias_tail = _bias_tile(rel_bias, 8, ck, ck)
    bias_far = _bias_tile(rel_bias, 8, LANE, past + LANE + FAR_DIST)
    bias_new = _bias_tile(rel_bias, 8, LANE, 0)
    new_t = lambda a: jnp.swapaxes(_pad_rows(a, ndb, t, LANE), 1, 2)

    qa, ka, va, sga, qb, ckv, kpe, kb, vb, sgb = _even_proj(x2, mod0, norm_g[0:1], even_w, rope, 1)
    nb = past // MOBA_BLOCK
    expand = jnp.asarray(np.arange(past)[None, :] // MOBA_BLOCK == np.arange(nb)[:, None], BF16)
    oa = _moba_sample(page_table, _stack_heads(qa, ndb, t, LANE), _pad_rows(ka, ndb, t, LANE),
                      _pad_rows(va, ndb, t, LANE), bias_tail, bias_far, bias_new, expand, cache_mk, cache_mv, t)
    oa = _unstack_heads(oa, ndb, t)
    wuk_full = w_uk.reshape(B_KV_RANK, N_HEADS * B_NOPE).astype(BF16)
    gk_nope = jnp.zeros((1, LANE), F32).at[0, 0:B_NOPE].set(gb_k[0:B_NOPE])
    qabs = _mla_absorb(qb, gk_nope, even_w[1])
    qabs = _stack_heads(qabs, ndb, t, B_KV_RANK)
    qpe = qb.reshape(n, N_HEADS, LANE)[:, :, B_NOPE:B_QK]
    half = B_ROPE // 2
    qpes = jnp.concatenate([qpe[..., half:], qpe[..., :half]], axis=-1)
    qpe = _stack_heads(qpe.reshape(n, -1), ndb, t, B_ROPE)
    qpes = _stack_heads(qpes.reshape(n, -1), ndb, t, B_ROPE)
    rsel = jnp.asarray(np.arange(N_HEADS * B_NOPE)[None, :] // B_NOPE == (np.arange(SROWS) // 8)[:, None], BF16)
    olat = _mla_sample(page_table, qabs, qpe, qpes, _pad_rows(ckv, ndb, t, LANE), new_t(kpe), wuk_full, rsel,
                       gb_k[B_NOPE:B_QK, None], _mla_rope_tables(past + LANE), cache_ckv, cache_kpe, t)
    ob = _mla_value(_unstack_heads(olat, ndb, t), even_w[2])
    wa, wb = out_even_w
    x2 = _out_proj(x2, mod0, [(oa, sga, wa), (ob, sgb, wb)], tm, 1)

    qc, kc, vc, sgc, qi, ki, kib, wi, kcb, vcb = _odd_proj(x2, mod1, norm_g[1:2], odd_w, tm, 1)
    qis = _stack_heads(qi.reshape(n, N_HEADS, LANE)[:, :, 0:IDX_DIM].reshape(n, -1), ndb, t, IDX_DIM)
    wis = jnp.broadcast_to(_stack_heads(wi[:, 0:IDX_HEADS], ndb, t, 1), (ndb, SROWS, LANE))
    scores, scores_new = _dsa_index(page_table, qis, wis, new_t(ki), cache_di, chunk_pages, t)
    mask, mask_new = _dsa_select(scores, scores_new, min(DSA_TOPK, (past + t) // 4), t)
    oc = _dsa_sample(page_table, _stack_heads(qc, ndb, t, LANE), mask, mask_new, _pad_rows(kc, ndb, t, LANE),
                     _pad_rows(vc, ndb, t, LANE), bias_tail, bias_far, bias_new, cache_dk, cache_dv)
    x2 = _out_proj(x2, mod1, [(_unstack_heads(oc, ndb, t), sgc, out_odd_w)], tm, 1)

    rows = (ka.reshape(1, ndb, t, N_KV_HEADS, A_HEAD_DIM), va.reshape(1, ndb, t, N_KV_HEADS, A_HEAD_DIM),
            ckv.reshape(1, ndb, t, B_KV_RANK), kpe.reshape(1, ndb, t, B_ROPE),
            kc.reshape(1, ndb, t, N_KV_HEADS, C_HEAD_DIM), vc.reshape(1, ndb, t, N_KV_HEADS, C_HEAD_DIM),
            ki.reshape(1, ndb, t, IDX_DIM))
    return x2.reshape(ndb, t, D_MODEL), rows


def kernel(x_prompt, x_sample, cache_moba_k, cache_moba_v, cache_mla_ckv, cache_mla_kpe, cache_dsa_k, cache_dsa_v,
           cache_dsa_kidx, page_table, c_prompt, c_sample, rel_bias, norm_g, ada_w, ada_b, w_in_even, ga_q, ga_k,
           gb_q, gb_k, g_ckv, w_uk, w_uv, w_out_even, w_in_odd, gc_q, gc_k, w_out_odd):
    nb, ndb = x_prompt.shape[0], x_sample.shape[0]
    pad = (-(nb + ndb)) % 8
    c_all = jnp.concatenate([c_prompt, c_sample, jnp.zeros((pad, D_MODEL), F32)], axis=0)
    mods = _modulation(c_all, ada_w, ada_b)
    tq = MOBA_BLOCK
    assert tq >= FAR_DIST
    rel_tiles = (_bias_tile(rel_bias, tq, tq, 0, shifted=True), _bias_tile(rel_bias, tq, tq, tq, shifted=True))
    even_w = _even_weights(w_in_even[0], ga_q[0], ga_k[0], gb_q[0], gb_k[0], g_ckv[0], w_uk[0], w_uv[0])
    out_even_w = _out_weights_even(w_out_even[0])
    odd_w = _odd_weights(w_in_odd[0], gc_q[0], gc_k[0])
    out_odd_w = w_out_odd[0].astype(BF16)
    y_p, rows_p = _prompt_trunk(x_prompt, mods, rel_tiles, norm_g, even_w, out_even_w, odd_w, out_odd_w)
    pool = cache_moba_k.shape[0] * cache_moba_k.shape[1]
    caches = (jnp.transpose(cache_moba_k, (0, 1, 3, 4, 2)).reshape(pool, LANE, PAGE),
              jnp.transpose(cache_moba_v, (0, 1, 3, 4, 2)).reshape(pool, LANE, PAGE),
              cache_mla_ckv.reshape(pool, PAGE, B_KV_RANK),
              jnp.transpose(cache_mla_kpe, (0, 1, 3, 2)).reshape(pool, B_ROPE, PAGE),
              cache_dsa_k.reshape(pool, N_KV_HEADS * PAGE, LANE), cache_dsa_v.reshape(pool, N_KV_HEADS * PAGE, LANE),
              jnp.transpose(cache_dsa_kidx, (0, 1, 3, 2)).reshape(pool, IDX_DIM, PAGE))
    y_s, rows_s = _sample_trunk(x_sample, mods, nb, rel_bias, norm_g, even_w, out_even_w, odd_w, out_odd_w,
                                page_table, caches, gb_k[0], w_uk[0])
    return (y_p, y_s) + rows_p + rows_s
```
